```python
import math
import jax, jax.numpy as jnp
from jax import lax
import numpy as np


D_MODEL = 2048
BATCH = 2
SEQ = 8192
DEPTH = 4
DEC_BATCH = 8
DEC_SEQ = 4096
PAST_LEN = 128

N_MIXERS = 3
N_A = len(range(0, DEPTH, N_MIXERS))
N_B = len(range(1, DEPTH, N_MIXERS))
N_C = len(range(2, DEPTH, N_MIXERS))
CONV_W = 31
CHUNK = 128
D_B = D_MODEL
H_B = 16
SSM_GROUP = 16
N_GROUPS = D_MODEL // SSM_GROUP
SSM_STATE = 64
DT_MIN = 0.001
DT_MAX = 0.1
N_MEM = 256
N_XHEADS = 4
XHEAD_DIM = D_MODEL // N_XHEADS
N_EXPERTS = 16
EC_CAPACITY = 2
D_EXPERT = 2 * D_MODEL
ALPHA = (2.0 * DEPTH) ** 0.25
BETA = (8.0 * DEPTH) ** -0.25
LN_EPS = 1e-5

kernel_name = 'hybrid_conv_gmlp_s5_ec_encoder'


def _layer_norm(x, g, b):
    xf = x.astype(jnp.float32)
    mu = jnp.mean(xf, axis=-1, keepdims=True)
    var = jnp.mean(jnp.square(xf - mu), axis=-1, keepdims=True)
    y = (xf - mu) * lax.rsqrt(var + LN_EPS)
    return (y * g.astype(jnp.float32) + b.astype(jnp.float32)).astype(x.dtype)


def _conv_module(x, w_in, b_in, dw, dw_b, ln_g, ln_b, w_out, b_out):
    h = x @ w_in + b_in
    a, g = jnp.split(h, 2, axis=-1)
    h = a * jax.nn.sigmoid(g)
    h = lax.conv_general_dilated(
        h, dw[:, None, :], window_strides=(1,),
        padding=[(CONV_W // 2, CONV_W // 2)],
        dimension_numbers=('NWC', 'WIO', 'NWC'),
        feature_group_count=D_MODEL) + dw_b
    h = jax.nn.silu(_layer_norm(h, ln_g, ln_b))
    return h @ w_out + b_out


def _chunked_sgu(x, w_in, b_in, ln_g, ln_b, w_s, b_s, w_out, b_out):
    bsz, seq, _ = x.shape
    h = jax.nn.gelu(x @ w_in + b_in)
    u, v = jnp.split(h, 2, axis=-1)
    v = _layer_norm(v, ln_g, ln_b)
    v = v.reshape(bsz, seq // CHUNK, CHUNK, H_B, D_B // H_B)
    s = jnp.einsum('hqp,bnphc->bnqhc', w_s, v) + b_s.T[:, :, None]
    s = s.reshape(bsz, seq, D_B)
    return (u * s) @ w_out + b_out


def _s5_direction(xg, a_re, a_im, b_re, b_im, c_re, c_im, log_dt, reverse):
    f32 = jnp.float32
    a_re = a_re.astype(f32); a_im = a_im.astype(f32)
    b_re = b_re.astype(f32); b_im = b_im.astype(f32)
    c_re = c_re.astype(f32); c_im = c_im.astype(f32)
    dt = jnp.exp(log_dt.astype(f32))[:, None]
    mag = jnp.exp(a_re * dt)
    lb_re = mag * jnp.cos(a_im * dt)
    lb_im = mag * jnp.sin(a_im * dt)
    den = a_re * a_re + a_im * a_im
    n_re = lb_re - 1.0
    n_im = lb_im
    f_re = (n_re * a_re + n_im * a_im) / den
    f_im = (n_im * a_re - n_re * a_im) / den
    bb_re = f_re[..., None] * b_re - f_im[..., None] * b_im
    bb_im = f_re[..., None] * b_im + f_im[..., None] * b_re
    bu_re = jnp.einsum('bsgc,gpc->bsgp', xg, bb_re)
    bu_im = jnp.einsum('bsgc,gpc->bsgp', xg, bb_im)
    if reverse:
        bu_re = jnp.flip(bu_re, axis=1)
        bu_im = jnp.flip(bu_im, axis=1)
    la_re = jnp.broadcast_to(lb_re, bu_re.shape)
    la_im = jnp.broadcast_to(lb_im, bu_im.shape)

    def combine(e1, e2):
        a1r, a1i, b1r, b1i = e1
        a2r, a2i, b2r, b2i = e2
        return (a2r * a1r - a2i * a1i,
                a2r * a1i + a2i * a1r,
                a2r * b1r - a2i * b1i + b2r,
                a2r * b1i + a2i * b1r + b2i)

    _, _, h_re, h_im = lax.associative_scan(combine, (la_re, la_im, bu_re, bu_im), axis=1)
    if reverse:
        h_re = jnp.flip(h_re, axis=1)
        h_im = jnp.flip(h_im, axis=1)
    return jnp.einsum('bsgp,gcp->bsgc', h_re, c_re) - jnp.einsum('bsgp,gcp->bsgc', h_im, c_im)


def _s5_mixer(x, a_re, a_im, b_re, b_im, c_re, c_im, log_dt, d_skip, w_glu, b_glu):
    bsz, seq, _ = x.shape
    xf = x.astype(jnp.float32)
    xg = xf.reshape(bsz, seq, N_GROUPS, SSM_GROUP)
    y = (_s5_direction(xg, a_re[0], a_im[0], b_re[0], b_im[0], c_re[0], c_im[0], log_dt[0], False)
         + _s5_direction(xg, a_re[1], a_im[1], b_re[1], b_im[1], c_re[1], c_im[1], log_dt[1], True))
    y = y.reshape(bsz, seq, D_MODEL) + d_skip.astype(jnp.float32) * xf
    y = jax.nn.gelu(y).astype(x.dtype)
    val, gate = jnp.split(y @ w_glu + b_glu, 2, axis=-1)
    return val * jax.nn.sigmoid(gate)


def _memory_xattn(x, mem, wq, wk, wv, wo):
    bsz, seq, _ = x.shape
    n_mem = mem.shape[1]
    q = (x @ wq).reshape(bsz, seq, N_XHEADS, XHEAD_DIM)
    k = (mem @ wk).reshape(bsz, n_mem, N_XHEADS, XHEAD_DIM)
    v = (mem @ wv).reshape(bsz, n_mem, N_XHEADS, XHEAD_DIM)
    s = jnp.einsum('bshd,bmhd->bhsm', q, k).astype(jnp.float32) * (XHEAD_DIM ** -0.5)
    p = jax.nn.softmax(s, axis=-1).astype(x.dtype)
    o = jnp.einsum('bhsm,bmhd->bshd', p, v).reshape(bsz, seq, D_MODEL)
    return o @ wo


def _ec_moe(x, w_router, w_gate, w_up, w_down):
    bsz, seq, dm = x.shape
    xt = x.reshape(-1, dm)
    n_tok = xt.shape[0]
    cap = EC_CAPACITY * n_tok // N_EXPERTS
    probs = jax.nn.softmax((xt @ w_router).astype(jnp.float32), axis=-1)
    gate, idx = lax.top_k(probs.T, cap)
    xe = jnp.take(xt, idx, axis=0)
    h = jax.nn.silu(jnp.einsum('ecd,edf->ecf', xe, w_gate)) * jnp.einsum('ecd,edf->ecf', xe, w_up)
    ye = jnp.einsum('ecf,efd->ecd', h, w_down) * gate[..., None].astype(x.dtype)
    out = jnp.zeros_like(xt).at[idx.reshape(-1)].add(ye.reshape(-1, dm))
    return out.reshape(bsz, seq, dm)


def _trunk(x, mem, p):
    for i in range(DEPTH):
        m = i % N_MIXERS
        j = i // N_MIXERS
        if m == 0:
            t = _conv_module(x, p['a_w_in'][j], p['a_b_in'][j], p['a_dw'][j], p['a_dw_b'][j],
                             p['a_ln_g'][j], p['a_ln_b'][j], p['a_w_out'][j], p['a_b_out'][j])
        elif m == 1:
            t = _chunked_sgu(x, p['b_w_in'][j], p['b_b_in'][j], p['b_ln_g'][j], p['b_ln_b'][j],
                             p['b_w_s'][j], p['b_b_s'][j], p['b_w_out'][j], p['b_b_out'][j])
        else:
            t = _s5_mixer(x, p['c_a_re'][j], p['c_a_im'][j], p['c_b_re'][j], p['c_b_im'][j],
                          p['c_c_re'][j], p['c_c_im'][j], p['c_log_dt'][j], p['c_d'][j],
                          p['c_w_glu'][j], p['c_b_glu'][j])
        x = _layer_norm(ALPHA * x + t, p['ln_g'][i, 0], p['ln_b'][i, 0])
        t = _memory_xattn(x, mem, p['x_wq'][i], p['x_wk'][i], p['x_wv'][i], p['x_wo'][i])
        x = _layer_norm(ALPHA * x + t, p['ln_g'][i, 1], p['ln_b'][i, 1])
        t = _ec_moe(x, p['e_w_router'][i], p['e_w_gate'][i], p['e_w_up'][i], p['e_w_down'][i])
        x = _layer_norm(ALPHA * x + t, p['ln_g'][i, 2], p['ln_b'][i, 2])
    return x


def setup_inputs(seed: int = 0) -> dict:
    key = jax.random.key(seed)
    ks = iter(jax.random.split(key, 64))
    f32 = jnp.float32
    D = D_MODEL
    G = N_GROUPS
    P = SSM_STATE
    GW = SSM_GROUP

    def nrm(shape, scale):
        return scale * jax.random.normal(next(ks), shape, f32)

    out = {}
    out['x_prompt'] = nrm((BATCH, SEQ, D), 1.0)
    out['x_sample'] = nrm((DEC_BATCH, DEC_SEQ, D), 1.0)
    out['mem_prompt'] = nrm((BATCH, N_MEM, D), 1.0)
    out['mem_sample'] = nrm((DEC_BATCH, N_MEM, D), 1.0)
    out['a_w_in'] = nrm((N_A, D, 2 * D), D ** -0.5)
    out['a_b_in'] = nrm((N_A, 2 * D), 0.01)
    out['a_dw'] = nrm((N_A, CONV_W, D), CONV_W ** -0.5)
    out['a_dw_b'] = nrm((N_A, D), 0.01)
    out['a_ln_g'] = 1.0 + nrm((N_A, D), 0.01)
    out['a_ln_b'] = nrm((N_A, D), 0.01)
    out['a_w_out'] = nrm((N_A, D, D), D ** -0.5 * BETA)
    out['a_b_out'] = nrm((N_A, D), 0.01)
    out['b_w_in'] = nrm((N_B, D, 2 * D_B), D ** -0.5)
    out['b_b_in'] = nrm((N_B, 2 * D_B), 0.01)
    out['b_ln_g'] = 1.0 + nrm((N_B, D_B), 0.01)
    out['b_ln_b'] = nrm((N_B, D_B), 0.01)
    out['b_w_s'] = nrm((N_B, H_B, CHUNK, CHUNK), CHUNK ** -0.5)
    out['b_b_s'] = 1.0 + nrm((N_B, H_B, CHUNK), 0.01)
    out['b_w_out'] = nrm((N_B, D_B, D), D_B ** -0.5 * BETA)
    out['b_b_out'] = nrm((N_B, D), 0.01)
    out['c_a_re'] = -0.5 + nrm((N_C, 2, G, P), 0.01)
    out['c_a_im'] = jnp.broadcast_to(math.pi * jnp.arange(P, dtype=f32), (N_C, 2, G, P)) + nrm((N_C, 2, G, P), 0.01)
    out['c_b_re'] = nrm((N_C, 2, G, P, GW), (2.0 * GW) ** -0.5)
    out['c_b_im'] = nrm((N_C, 2, G, P, GW), (2.0 * GW) ** -0.5)
    out['c_c_re'] = nrm((N_C, 2, G, GW, P), (2.0 * P) ** -0.5)
    out['c_c_im'] = nrm((N_C, 2, G, GW, P), (2.0 * P) ** -0.5)
    out['c_log_dt'] = jax.random.uniform(next(ks), (N_C, 2, G), f32, math.log(DT_MIN), math.log(DT_MAX))
    out['c_d'] = nrm((N_C, D), 1.0)
    out['c_w_glu'] = jnp.concatenate([nrm((N_C, D, D), D ** -0.5 * BETA), nrm((N_C, D, D), D ** -0.5)], axis=-1)
    out['c_b_glu'] = nrm((N_C, 2 * D), 0.01)
    out['x_wq'] = nrm((DEPTH, D, D), D ** -0.5)
    out['x_wk'] = nrm((DEPTH, D, D), D ** -0.5)
    out['x_wv'] = nrm((DEPTH, D, D), D ** -0.5)
    out['x_wo'] = nrm((DEPTH, D, D), D ** -0.5 * BETA)
    out['e_w_router'] = nrm((DEPTH, D, N_EXPERTS), D ** -0.5)
    out['e_w_gate'] = nrm((DEPTH, N_EXPERTS, D, D_EXPERT), D ** -0.5)
    out['e_w_up'] = nrm((DEPTH, N_EXPERTS, D, D_EXPERT), D ** -0.5)
    out['e_w_down'] = nrm((DEPTH, N_EXPERTS, D_EXPERT, D), D_EXPERT ** -0.5 * BETA)
    out['ln_g'] = 1.0 + nrm((DEPTH, 3, D), 0.01)
    out['ln_b'] = nrm((DEPTH, 3, D), 0.01)
    return out


def reference(x_prompt, x_sample, mem_prompt, mem_sample,
              a_w_in, a_b_in, a_dw, a_dw_b, a_ln_g, a_ln_b, a_w_out, a_b_out,
              b_w_in, b_b_in, b_ln_g, b_ln_b, b_w_s, b_b_s, b_w_out, b_b_out,
              c_a_re, c_a_im, c_b_re, c_b_im, c_c_re, c_c_im, c_log_dt, c_d, c_w_glu, c_b_glu,
              x_wq, x_wk, x_wv, x_wo,
              e_w_router, e_w_gate, e_w_up, e_w_down,
              ln_g, ln_b):
    p = dict(a_w_in=a_w_in, a_b_in=a_b_in, a_dw=a_dw, a_dw_b=a_dw_b, a_ln_g=a_ln_g, a_ln_b=a_ln_b,
             a_w_out=a_w_out, a_b_out=a_b_out,
             b_w_in=b_w_in, b_b_in=b_b_in, b_ln_g=b_ln_g, b_ln_b=b_ln_b, b_w_s=b_w_s, b_b_s=b_b_s,
             b_w_out=b_w_out, b_b_out=b_b_out,
             c_a_re=c_a_re, c_a_im=c_a_im, c_b_re=c_b_re, c_b_im=c_b_im, c_c_re=c_c_re, c_c_im=c_c_im,
             c_log_dt=c_log_dt, c_d=c_d, c_w_glu=c_w_glu, c_b_glu=c_b_glu,
             x_wq=x_wq, x_wk=x_wk, x_wv=x_wv, x_wo=x_wo,
             e_w_router=e_w_router, e_w_gate=e_w_gate, e_w_up=e_w_up, e_w_down=e_w_down,
             ln_g=ln_g, ln_b=ln_b)
    y_prompt = _trunk(x_prompt, mem_prompt, p)
    y_sample = _trunk(x_sample, mem_sample, p)
    return (y_prompt, y_sample)
```

```python
import functools
import math

import jax
import jax.numpy as jnp
from jax import lax
from jax.experimental import pallas as pl
from jax.experimental.pallas import tpu as pltpu

F32 = jnp.float32
BF16 = jnp.bfloat16
LN_EPS = 1e-5
DT_HALO = 16
S5_CHUNK = 16
VMEM_LIMIT = 56 * 1024 * 1024


def _cparams(sem):
    return pltpu.CompilerParams(dimension_semantics=sem, vmem_limit_bytes=VMEM_LIMIT)


def _resident(shape):
    nd = len(shape)
    return pl.BlockSpec(shape, lambda *_: (0,) * nd, pipeline_mode=pl.Buffered(1))


def _ln(z, g, b):
    mu = jnp.mean(z, axis=-1, keepdims=True)
    zc = z - mu
    var = jnp.mean(zc * zc, axis=-1, keepdims=True)
    return zc * lax.rsqrt(var + LN_EPS) * g + b


def _row(v):
    return v.reshape(1, -1).astype(F32)


def _pick(n, pref):
    t = min(n, pref)
    while n % t:
        t //= 2
    return t


def _glu_in_kernel(x_ref, wa_ref, wg_ref, ba_ref, bg_ref, o_ref):
    xb = x_ref[...].astype(BF16)
    a = jnp.dot(xb, wa_ref[...], preferred_element_type=F32) + ba_ref[...]
    g = jnp.dot(xb, wg_ref[...], preferred_element_type=F32) + bg_ref[...]
    o_ref[...] = a * jax.nn.sigmoid(g)


def _glu_in(x, w, b):
    m, d = x.shape
    h = w.shape[1] // 2
    tm, tn = _pick(m, 1024), _pick(h, 512)
    nj = h // tn
    b = _row(b)
    return pl.pallas_call(
        _glu_in_kernel,
        grid=(m // tm, nj),
        in_specs=[pl.BlockSpec((tm, d), lambda i, j: (i, 0)),
                  pl.BlockSpec((d, tn), lambda i, j: (0, j)),
                  pl.BlockSpec((d, tn), lambda i, j: (0, j + nj)),
                  pl.BlockSpec((1, tn), lambda i, j: (0, j)),
                  pl.BlockSpec((1, tn), lambda i, j: (0, j + nj))],
        out_specs=pl.BlockSpec((tm, tn), lambda i, j: (i, j)),
        out_shape=jax.ShapeDtypeStruct((m, h), F32),
        compiler_params=_cparams(("parallel", "arbitrary")),
    )(x, w, w, b, b)


def _gelu_in_kernel(x_ref, w_ref, b_ref, o_ref):
    xb = x_ref[...].astype(BF16)
    o_ref[...] = jax.nn.gelu(jnp.dot(xb, w_ref[...], preferred_element_type=F32) + b_ref[...])


def _gelu_in(x, w, b):
    m, d = x.shape
    n = w.shape[1]
    tm, tn = _pick(m, 1024), _pick(n, 512)
    return pl.pallas_call(
        _gelu_in_kernel,
        grid=(m // tm, n // tn),
        in_specs=[pl.BlockSpec((tm, d), lambda i, j: (i, 0)),
                  pl.BlockSpec((d, tn), lambda i, j: (0, j)),
                  pl.BlockSpec((1, tn), lambda i, j: (0, j))],
        out_specs=pl.BlockSpec((tm, tn), lambda i, j: (i, j)),
        out_shape=jax.ShapeDtypeStruct((m, n), F32),
        compiler_params=_cparams(("parallel", "arbitrary")),
    )(x, w, _row(b))


def _mm_kernel(x_ref, w_ref, o_ref):
    o_ref[...] = jnp.dot(x_ref[...].astype(BF16), w_ref[...], preferred_element_type=F32).astype(o_ref.dtype)


def _mm(x, w, out_dtype):
    m, d = x.shape
    n = w.shape[1]
    tm, tn = _pick(m, 512), _pick(n, 512)
    return pl.pallas_call(
        _mm_kernel,
        grid=(m // tm, n // tn),
        in_specs=[pl.BlockSpec((tm, d), lambda i, j: (i, 0)),
                  pl.BlockSpec((d, tn), lambda i, j: (0, j))],
        out_specs=pl.BlockSpec((tm, tn), lambda i, j: (i, j)),
        out_shape=jax.ShapeDtypeStruct((m, n), out_dtype),
        compiler_params=_cparams(("parallel", "arbitrary")),
    )(x, w)


def _conv_out_kernel(hp_ref, hc_ref, hn_ref, x_ref, dw_ref, dwb_ref, ag_ref, ab_ref, w_ref, bo_ref,
                     g_ref, b_ref, o_ref, hext_ref, cv_ref, *, tiles_per_seq, alpha, taps):
    i = pl.program_id(0)
    ts, d = hc_ref.shape
    first = (i % tiles_per_seq) == 0
    last = (i % tiles_per_seq) == tiles_per_seq - 1
    hext_ref[0:DT_HALO, :] = jnp.where(first, 0.0, hp_ref[...])
    hext_ref[DT_HALO:DT_HALO + ts, :] = hc_ref[...]
    hext_ref[DT_HALO + ts:, :] = jnp.where(last, 0.0, hn_ref[...])
    base = DT_HALO - taps // 2

    def col_block(c, carry):
        cols = pl.ds(pl.multiple_of(c * 128, 128), 128)
        acc = jnp.zeros((ts, 128), F32)
        for k in range(taps):
            acc = acc + hext_ref[pl.ds(base + k, ts), cols] * dw_ref[pl.ds(k, 1), cols]
        cv_ref[:, cols] = acc
        return carry

    lax.fori_loop(0, d // 128, col_block, 0)
    z = _ln(cv_ref[...] + dwb_ref[...], ag_ref[...], ab_ref[...])
    z = (z * jax.nn.sigmoid(z)).astype(BF16)
    t = jnp.dot(z, w_ref[...], preferred_element_type=F32) + bo_ref[...]
    o_ref[...] = _ln(alpha * x_ref[...] + t, g_ref[...], b_ref[...])


def _conv_out(h, x, seq, dw, dw_b, a_g, a_b, w_out, b_out, g, b, alpha):
    m, d = x.shape
    taps = dw.shape[0]
    assert taps // 2 <= DT_HALO
    ts = _pick(seq, 256)
    hb = ts // DT_HALO
    nhalo = m // DT_HALO
    kern = functools.partial(_conv_out_kernel, tiles_per_seq=seq // ts, alpha=alpha, taps=taps)
    return pl.pallas_call(
        kern,
        grid=(m // ts,),
        in_specs=[pl.BlockSpec((DT_HALO, d), lambda i: (jnp.maximum(i * hb - 1, 0), 0)),
                  pl.BlockSpec((ts, d), lambda i: (i, 0)),
                  pl.BlockSpec((DT_HALO, d), lambda i: (jnp.minimum((i + 1) * hb, nhalo - 1), 0)),
                  pl.BlockSpec((ts, d), lambda i: (i, 0)),
                  _resident(dw.shape), _resident((1, d)), _resident((1, d)), _resident((1, d)),
                  _resident(w_out.shape), _resident((1, d)), _resident((1, d)), _resident((1, d))],
        out_specs=pl.BlockSpec((ts, d), lambda i: (i, 0)),
        out_shape=jax.ShapeDtypeStruct((m, d), F32),
        scratch_shapes=[pltpu.VMEM((ts + 2 * DT_HALO, d), F32), pltpu.VMEM((ts, d), F32)],
        compiler_params=_cparams(("parallel",)),
    )(h, h, h, x, dw, _row(dw_b), _row(a_g), _row(a_b), w_out, _row(b_out), _row(g), _row(b))


def _sgu_out_kernel(u_ref, v_ref, x_ref, vg_ref, vb_ref, ws_ref, bs_ref, w_ref, bo_ref, g_ref, b_ref,
                    o_ref, z_ref, *, alpha):
    tm, d = u_ref.shape
    nh, chunk, _ = ws_ref.shape
    hw = d // nh
    vn = _ln(v_ref[...], vg_ref[...], vb_ref[...]).astype(BF16)
    for r in range(tm // chunk):
        rows = slice(r * chunk, (r + 1) * chunk)
        for hd in range(nh):
            cols = slice(hd * hw, (hd + 1) * hw)
            s = jnp.dot(ws_ref[hd], vn[rows, cols], preferred_element_type=F32) + bs_ref[hd]
            z_ref[rows, cols] = (u_ref[rows, cols] * s).astype(BF16)
    t = jnp.dot(z_ref[...], w_ref[...], preferred_element_type=F32) + bo_ref[...]
    o_ref[...] = _ln(alpha * x_ref[...] + t, g_ref[...], b_ref[...])


def _sgu_out(h, x, v_g, v_b, w_s, b_s, w_out, b_out, g, b, alpha):
    m, d = x.shape
    nh, chunk, _ = w_s.shape
    tm = 2 * chunk
    assert m % tm == 0
    bs3 = b_s.astype(F32)[:, :, None]
    return pl.pallas_call(
        functools.partial(_sgu_out_kernel, alpha=alpha),
        grid=(m // tm,),
        in_specs=[pl.BlockSpec((tm, d), lambda i: (i, 0)),
                  pl.BlockSpec((tm, d), lambda i: (i, 1)),
                  pl.BlockSpec((tm, d), lambda i: (i, 0)),
                  _resident((1, d)), _resident((1, d)), _resident(w_s.shape), _resident(bs3.shape),
                  _resident(w_out.shape), _resident((1, d)), _resident((1, d)), _resident((1, d))],
        out_specs=pl.BlockSpec((tm, d), lambda i: (i, 0)),
        out_shape=jax.ShapeDtypeStruct((m, d), F32),
        scratch_shapes=[pltpu.VMEM((tm, d), BF16)],
        compiler_params=_cparams(("parallel",)),
    )(h, h, x, _row(v_g), _row(v_b), w_s, bs3, w_out, _row(b_out), _row(g), _row(b))


def _s5_weights(a_re, a_im, b_re, b_im, c_re, c_im, log_dt):
    hp = lax.Precision.HIGHEST
    L = S5_CHUNK
    _, G, P = a_re.shape
    GW = b_re.shape[-1]
    a_re = a_re.astype(F32); a_im = a_im.astype(F32)
    dt = jnp.exp(log_dt.astype(F32))[..., None]
    n = jnp.arange(L + 1, dtype=F32)[:, None, None, None]
    mag = jnp.exp(n * (a_re * dt)[None])
    ang = n * (a_im * dt)[None]
    pw_re = mag * jnp.cos(ang)
    pw_im = mag * jnp.sin(ang)
    lb_re, lb_im = pw_re[1], pw_im[1]
    den = a_re * a_re + a_im * a_im
    n_re = lb_re - 1.0
    f_re = (n_re * a_re + lb_im * a_im) / den
    f_im = (lb_im * a_re - n_re * a_im) / den
    bb_re = f_re[..., None] * b_re - f_im[..., None] * b_im
    bb_im = f_re[..., None] * b_im + f_im[..., None] * b_re
    c_re = c_re.astype(F32); c_im = c_im.astype(F32)
    ct_re = c_re.transpose(0, 1, 3, 2)[..., None]
    ct_im = c_im.transpose(0, 1, 3, 2)[..., None]
    cb_re = ct_re * bb_re[..., None, :] - ct_im * bb_im[..., None, :]
    cb_im = ct_re * bb_im[..., None, :] + ct_im * bb_re[..., None, :]
    kern = (jnp.einsum('ndgp,dgpck->dgnck', pw_re[:L], cb_re, precision=hp)
            - jnp.einsum('ndgp,dgpck->dgnck', pw_im[:L], cb_im, precision=hp))
    s_i = jnp.arange(L)[:, None]
    t_i = jnp.arange(L)[None, :]
    lag_f = t_i - s_i
    lag_b = s_i - t_i
    tf = jnp.where((lag_f >= 0)[None, :, :, None, None], kern[0][:, jnp.clip(lag_f, 0, L - 1)], 0.0)
    tb = jnp.where((lag_b >= 0)[None, :, :, None, None], kern[1][:, jnp.clip(lag_b, 0, L - 1)], 0.0)
    toep = (tf + tb).transpose(0, 1, 4, 2, 3).reshape(G, L * GW, L * GW)
    ef = jnp.arange(L - 1, -1, -1)
    eb = jnp.arange(L)

    def inc(d, e):
        pr, pi = pw_re[e, d], pw_im[e, d]
        re = pr[..., None] * bb_re[d][None] - pi[..., None] * bb_im[d][None]
        im = pr[..., None] * bb_im[d][None] + pi[..., None] * bb_re[d][None]
        to = lambda z: z.transpose(1, 0, 3, 2).reshape(G, L * GW, P)
        return to(re), to(im)

    sf_re, sf_im = inc(0, ef)
    sb_re, sb_im = inc(1, eb)
    w1g = jnp.stack([sf_re, sf_im, sb_re, sb_im], axis=2)
    of = jnp.arange(1, L + 1)
    ob = jnp.arange(L, 0, -1)

    def outm(d, e):
        pr, pi = pw_re[e, d], pw_im[e, d]
        re = c_re[d][None] * pr[:, :, None, :] - c_im[d][None] * pi[:, :, None, :]
        im = c_re[d][None] * pi[:, :, None, :] + c_im[d][None] * pr[:, :, None, :]
        to = lambda z: z.transpose(1, 3, 0, 2).reshape(G, P, L * GW)
        return to(re), to(-im)

    mf_re, mf_im = outm(0, of)
    mb_re, mb_im = outm(1, ob)
    w2g = jnp.stack([mf_re, mf_im, mb_re, mb_im], axis=1)
    eye = jnp.eye(2, dtype=F32)
    q = G // 2
    r = L * GW
    w1 = jnp.einsum('qhrpk,hj->qhrpjk', w1g.reshape(q, 2, r, 4, P), eye).reshape(q, 2 * r, 8 * P)
    tp = jnp.einsum('qhrc,hj->qhrjc', toep.reshape(q, 2, r, r), eye).reshape(q, 2 * r, 2 * r)
    w2s = jnp.einsum('qhpkc,hj->qphkjc', w2g.reshape(q, 2, 4, P, r), eye).reshape(q, 8 * P, 2 * r)
    w2 = jnp.concatenate([tp, w2s], axis=1)
    lam_g = jnp.stack([pw_re[L, 0], pw_im[L, 0], pw_re[L, 1], pw_im[L, 1]], axis=1)
    lam = lam_g.reshape(q, 2, 4, P).transpose(0, 2, 1, 3).reshape(q, 8 * P)
    return w1.astype(BF16), w2.astype(BF16), lam


def _s5_inc_kernel(x_ref, w_ref, o_ref):
    o_ref[...] = jnp.dot(x_ref[0], w_ref[0], preferred_element_type=F32)


def _s5_mix_kernel(x_ref, h_ref, w_ref, o_ref):
    kx = x_ref.shape[2]
    y = jnp.dot(x_ref[0], w_ref[0, :kx, :], preferred_element_type=F32)
    y = y + jnp.dot(h_ref[...].astype(BF16), w_ref[0, kx:, :], preferred_element_type=F32)
    o_ref[0] = y


def _s5_scan_kernel(s_ref, lam_ref, h_ref, *, pairs, lanes):
    nc = s_ref.shape[0]
    sub = 8
    pl_w = lanes // 4
    zero = jnp.zeros((1, pl_w), F32)

    def step(kk, carry):
        rf = pl.ds(pl.multiple_of(kk * sub, sub), sub)
        rb = pl.ds(pl.multiple_of(nc - sub - kk * sub, sub), sub)
        new = []
        for j in range(pairs):
            o = j * lanes
            cf_r, cf_i = slice(o, o + pl_w), slice(o + pl_w, o + 2 * pl_w)
            cb_r, cb_i = slice(o + 2 * pl_w, o + 3 * pl_w), slice(o + 3 * pl_w, o + 4 * pl_w)
            fr, fi, br, bi = carry[4 * j:4 * j + 4]
            lfr, lfi, lbr, lbi = lam_ref[:, cf_r], lam_ref[:, cf_i], lam_ref[:, cb_r], lam_ref[:, cb_i]
            sfr, sfi, sbr, sbi = s_ref[rf, cf_r], s_ref[rf, cf_i], s_ref[rb, cb_r], s_ref[rb, cb_i]
            ofr, ofi, obr, obi = [], [], [], []
            for r in range(sub):
                ofr.append(fr)
                ofi.append(fi)
                fr, fi = lfr * fr - lfi * fi + sfr[r:r + 1], lfr * fi + lfi * fr + sfi[r:r + 1]
            for r in range(sub - 1, -1, -1):
                obr.append(br)
                obi.append(bi)
                br, bi = lbr * br - lbi * bi + sbr[r:r + 1], lbr * bi + lbi * br + sbi[r:r + 1]
            h_ref[rf, cf_r] = jnp.concatenate(ofr, axis=0)
            h_ref[rf, cf_i] = jnp.concatenate(ofi, axis=0)
            h_ref[rb, cb_r] = jnp.concatenate(obr[::-1], axis=0)
            h_ref[rb, cb_i] = jnp.concatenate(obi[::-1], axis=0)
            new += [fr, fi, br, bi]
        return tuple(new)

    lax.fori_loop(0, nc // sub, step, (zero,) * (4 * pairs))


def _s5_core(x, bsz, seq, w1, w2, lam):
    m, d = x.shape
    L = S5_CHUNK
    q, kx, ns = w1.shape
    gw = kx // (2 * L)
    nch = m // L
    nc = seq // L
    xp = x.astype(BF16).reshape(nch, L, q, 2, gw).transpose(2, 0, 3, 1, 4).reshape(q, nch, kx)
    tm = _pick(nch, 512)
    s = pl.pallas_call(
        _s5_inc_kernel,
        grid=(q, nch // tm),
        in_specs=[pl.BlockSpec((1, tm, kx), lambda p, i: (p, i, 0)),
                  pl.BlockSpec((1, kx, ns), lambda p, i: (p, 0, 0))],
        out_specs=pl.BlockSpec((tm, ns), lambda p, i: (i, p)),
        out_shape=jax.ShapeDtypeStruct((nch, q * ns), F32),
        compiler_params=_cparams(("parallel", "arbitrary")),
    )(xp, w1)
    pb = _pick(q, 4)
    h = pl.pallas_call(
        functools.partial(_s5_scan_kernel, pairs=pb, lanes=ns),
        grid=(bsz, q // pb),
        in_specs=[pl.BlockSpec((nc, pb * ns), lambda b, p: (b, p)),
                  pl.BlockSpec((1, pb * ns), lambda b, p: (0, p))],
        out_specs=pl.BlockSpec((nc, pb * ns), lambda b, p: (b, p)),
        out_shape=jax.ShapeDtypeStruct((nch, q * ns), F32),
        compiler_params=_cparams(("parallel", "arbitrary")),
    )(s, lam.reshape(1, q * ns))
    yp = pl.pallas_call(
        _s5_mix_kernel,
        grid=(q, nch // tm),
        in_specs=[pl.BlockSpec((1, tm, kx), lambda p, i: (p, i, 0)),
                  pl.BlockSpec((tm, ns), lambda p, i: (i, p)),
                  pl.BlockSpec((1, kx + ns, kx), lambda p, i: (p, 0, 0))],
        out_specs=pl.BlockSpec((1, tm, kx), lambda p, i: (p, i, 0)),
        out_shape=jax.ShapeDtypeStruct((q, nch, kx), F32),
        compiler_params=_cparams(("parallel", "arbitrary")),
    )(xp, h, w2)
    return yp.reshape(q, nch, 2, L, gw).transpose(1, 3, 0, 2, 4).reshape(m, d)


def _s5_out_kernel(y_ref, x_ref, d_ref, wv_ref, wg_ref, bv_ref, bg_ref, g_ref, b_ref, o_ref, z_ref, t_ref,
                   *, alpha, nj):
    j = pl.program_id(1)
    tn = wv_ref.shape[1]

    @pl.when(j == 0)
    def _():
        z_ref[...] = jax.nn.gelu(y_ref[...] + d_ref[...] * x_ref[...]).astype(BF16)

    zb = z_ref[...]
    val = jnp.dot(zb, wv_ref[...], preferred_element_type=F32) + bv_ref[...]
    gate = jnp.dot(zb, wg_ref[...], preferred_element_type=F32) + bg_ref[...]
    t_ref[:, pl.ds(pl.multiple_of(j * tn, tn), tn)] = val * jax.nn.sigmoid(gate)

    @pl.when(j == nj - 1)
    def _():
        o_ref[...] = _ln(alpha * x_ref[...] + t_ref[...], g_ref[...], b_ref[...])


def _s5_out(y, x, dskip, w_glu, b_glu, g, b, alpha):
    m, d = x.shape
    tm, tn = _pick(m, 512), _pick(d, 512)
    nj = d // tn
    bg = _row(b_glu)
    return pl.pallas_call(
        functools.partial(_s5_out_kernel, alpha=alpha, nj=nj),
        grid=(m // tm, nj),
        in_specs=[pl.BlockSpec((tm, d), lambda i, j: (i, 0)),
                  pl.BlockSpec((tm, d), lambda i, j: (i, 0)),
                  pl.BlockSpec((1, d), lambda i, j: (0, 0)),
                  pl.BlockSpec((d, tn), lambda i, j: (0, j)),
                  pl.BlockSpec((d, tn), lambda i, j: (0, j + nj)),
                  pl.BlockSpec((1, tn), lambda i, j: (0, j)),
                  pl.BlockSpec((1, tn), lambda i, j: (0, j + nj)),
                  pl.BlockSpec((1, d), lambda i, j: (0, 0)),
                  pl.BlockSpec((1, d), lambda i, j: (0, 0))],
        out_specs=pl.BlockSpec((tm, d), lambda i, j: (i, 0)),
        out_shape=jax.ShapeDtypeStruct((m, d), F32),
        scratch_shapes=[pltpu.VMEM((tm, d), BF16), pltpu.VMEM((tm, d), F32)],
        compiler_params=_cparams(("parallel", "arbitrary")),
    )(y, x, _row(dskip), w_glu, w_glu, bg, bg, _row(g), _row(b))


def _xattn_kernel(x_ref, k_ref, v_ref, wq_ref, wo_ref, g_ref, b_ref, o_ref, a_ref, *, heads, alpha):
    x = x_ref[0]
    d = x.shape[1]
    dh = d // heads
    scale = dh ** -0.5
    xb = x.astype(BF16)
    for hd in range(heads):
        cols = slice(hd * dh, (hd + 1) * dh)
        qh = jnp.dot(xb, wq_ref[:, cols], preferred_element_type=F32).astype(BF16)
        s = lax.dot_general(qh, k_ref[0, :, cols], (((1,), (1,)), ((), ())), preferred_element_type=F32) * scale
        s = s - jnp.max(s, axis=-1, keepdims=True)
        e = jnp.exp(s)
        p = (e / jnp.sum(e, axis=-1, keepdims=True)).astype(BF16)
        a_ref[:, cols] = jnp.dot(p, v_ref[0, :, cols], preferred_element_type=F32).astype(BF16)
    t = jnp.dot(a_ref[...], wo_ref[...], preferred_element_type=F32)
    o_ref[0] = _ln(alpha * x + t, g_ref[...], b_ref[...])


def _xattn(x3, k3, v3, wq, wo, g, b, heads, alpha):
    bsz, seq, d = x3.shape
    nm = k3.shape[1]
    tm = _pick(seq, 512)
    return pl.pallas_call(
        functools.partial(_xattn_kernel, heads=heads, alpha=alpha),
        grid=(bsz, seq // tm),
        in_specs=[pl.BlockSpec((1, tm, d), lambda bi, i: (bi, i, 0)),
                  pl.BlockSpec((1, nm, d), lambda bi, i: (bi, 0, 0)),
                  pl.BlockSpec((1, nm, d), lambda bi, i: (bi, 0, 0)),
                  _resident(wq.shape), _resident(wo.shape), _resident((1, d)), _resident((1, d))],
        out_specs=pl.BlockSpec((1, tm, d), lambda bi, i: (bi, i, 0)),
        out_shape=jax.ShapeDtypeStruct((bsz, seq, d), F32),
        scratch_shapes=[pltpu.VMEM((tm, d), BF16)],
        compiler_params=_cparams(("parallel", "arbitrary")),
    )(x3, k3, v3, wq, wo, _row(g), _row(b))


def _router_kernel(x_ref, w_ref, o_ref, *, n_exp):
    logits = jnp.dot(x_ref[...], w_ref[...], preferred_element_type=F32, precision=lax.Precision.HIGHEST)
    lane = lax.broadcasted_iota(jnp.int32, logits.shape, 1)
    logits = jnp.where(lane < n_exp, logits, -jnp.inf)
    e = jnp.exp(logits - jnp.max(logits, axis=-1, keepdims=True))
    p = e / jnp.sum(e, axis=-1, keepdims=True)
    o_ref[...] = p.T[:o_ref.shape[0], :]


def _router(x, w_router):
    m, d = x.shape
    n_exp = w_router.shape[1]
    assert n_exp <= 128
    rows = -(-n_exp // 8) * 8
    wpad = jnp.zeros((d, 128), F32).at[:, :n_exp].set(w_router.astype(F32))
    tm = _pick(m, 512)
    out = pl.pallas_call(
        functools.partial(_router_kernel, n_exp=n_exp),
        grid=(m // tm,),
        in_specs=[pl.BlockSpec((tm, d), lambda i: (i, 0)), _resident((d, 128))],
        out_specs=pl.BlockSpec((rows, tm), lambda i: (0, i)),
        out_shape=jax.ShapeDtypeStruct((rows, m), F32),
        compiler_params=_cparams(("parallel",)),
    )(x, wpad)
    return out[:n_exp]


def _moe_ffn_kernel(idx_ref, x_hbm, gate_ref, wg_ref, wu_ref, wd_ref, o_ref, xg_ref, xb_ref, sem, *, cap, nf):
    e = pl.program_id(0)
    r = pl.program_id(1)
    f = pl.program_id(2)
    tm = xg_ref.shape[0]

    @pl.when(f == 0)
    def _():
        base = e * cap + r * tm

        def issue(i, c):
            pltpu.make_async_copy(x_hbm.at[pl.ds(idx_ref[base + i], 1)], xg_ref.at[pl.ds(i, 1)], sem).start()
            return c

        lax.fori_loop(0, tm, issue, 0)

        def drain(i, c):
            pltpu.make_async_copy(x_hbm.at[pl.ds(0, 1)], xg_ref.at[pl.ds(i, 1)], sem).wait()
            return c

        lax.fori_loop(0, tm, drain, 0)
        xb_ref[...] = xg_ref[...].astype(BF16)

    xb = xb_ref[...]
    hg = jnp.dot(xb, wg_ref[0].astype(BF16), preferred_element_type=F32)
    hu = jnp.dot(xb, wu_ref[0].astype(BF16), preferred_element_type=F32)
    hh = (hg * jax.nn.sigmoid(hg) * hu).astype(BF16)
    part = jnp.dot(hh, wd_ref[0].astype(BF16), preferred_element_type=F32)

    @pl.when(f == 0)
    def _():
        o_ref[...] = part

    @pl.when(f > 0)
    def _():
        o_ref[...] += part

    @pl.when(f == nf - 1)
    def _():
        o_ref[...] = o_ref[...] * gate_ref[...]


def _moe_ffn(x, idx, gate, w_gate, w_up, w_down):
    n, d = x.shape
    n_exp, cap = idx.shape
    dff = w_gate.shape[2]
    tm, tf = _pick(cap, 1024), _pick(dff, 256)
    nr, nf = cap // tm, dff // tf
    grid_spec = pltpu.PrefetchScalarGridSpec(
        num_scalar_prefetch=1,
        grid=(n_exp, nr, nf),
        in_specs=[pl.BlockSpec(memory_space=pl.ANY),
                  pl.BlockSpec((tm, 1), lambda e, r, f, idx: (e * nr + r, 0)),
                  pl.BlockSpec((1, d, tf), lambda e, r, f, idx: (e, 0, f)),
                  pl.BlockSpec((1, d, tf), lambda e, r, f, idx: (e, 0, f)),
                  pl.BlockSpec((1, tf, d), lambda e, r, f, idx: (e, f, 0))],
        out_specs=pl.BlockSpec((tm, d), lambda e, r, f, idx: (e * nr + r, 0)),
        scratch_shapes=[pltpu.VMEM((tm, d), F32), pltpu.VMEM((tm, d), BF16), pltpu.SemaphoreType.DMA(())],
    )
    return pl.pallas_call(
        functools.partial(_moe_ffn_kernel, cap=cap, nf=nf),
        grid_spec=grid_spec,
        out_shape=jax.ShapeDtypeStruct((n_exp * cap, d), F32),
        compiler_params=_cparams(("arbitrary", "arbitrary", "arbitrary")),
    )(idx.reshape(-1).astype(jnp.int32), x, gate.reshape(-1, 1).astype(F32), w_gate, w_up, w_down)


def _res_ln_kernel(x_ref, t_ref, g_ref, b_ref, o_ref, *, alpha):
    o_ref[...] = _ln(alpha * x_ref[...] + t_ref[...], g_ref[...], b_ref[...])


def _res_ln(x, t, g, b, alpha):
    m, d = x.shape
    tm = _pick(m, 512)
    return pl.pallas_call(
        functools.partial(_res_ln_kernel, alpha=alpha),
        grid=(m // tm,),
        in_specs=[pl.BlockSpec((tm, d), lambda i: (i, 0)), pl.BlockSpec((tm, d), lambda i: (i, 0)),
                  _resident((1, d)), _resident((1, d))],
        out_specs=pl.BlockSpec((tm, d), lambda i: (i, 0)),
        out_shape=jax.ShapeDtypeStruct((m, d), F32),
        compiler_params=_cparams(("parallel",)),
    )(x, t, _row(g), _row(b))


def _ec_moe(x, w_router, w_gate, w_up, w_down, g, b, alpha, ec_capacity):
    n, d = x.shape
    n_exp = w_router.shape[1]
    cap = ec_capacity * n // n_exp
    probs_t = _router(x, w_router)
    gate, idx = lax.top_k(probs_t, cap)
    ye = _moe_ffn(x, idx, gate, w_gate, w_up, w_down)
    t = jnp.zeros_like(x).at[idx.reshape(-1)].add(ye)
    return _res_ln(x, t, g, b, alpha)


EC_CAPACITY = 2
N_XHEADS = 4
N_MIXERS = 3


def _trunk(x3, mem3, p):
    bsz, seq, d = x3.shape
    m = bsz * seq
    depth = p['ln_g'].shape[0]
    alpha = (2.0 * depth) ** 0.25
    x = x3.reshape(m, d)
    mem = mem3.reshape(-1, d)
    for i in range(depth):
        mix, j = i % N_MIXERS, i // N_MIXERS
        g, b = p['ln_g'][i], p['ln_b'][i]
        if mix == 0:
            h = _glu_in(x, p['a_w_in'][j].astype(BF16), p['a_b_in'][j])
            x = _conv_out(h, x, seq, p['a_dw'][j].astype(F32), p['a_dw_b'][j], p['a_ln_g'][j], p['a_ln_b'][j],
                          p['a_w_out'][j].astype(BF16), p['a_b_out'][j], g[0], b[0], alpha)
        elif mix == 1:
            h = _gelu_in(x, p['b_w_in'][j].astype(BF16), p['b_b_in'][j])
            x = _sgu_out(h, x, p['b_ln_g'][j], p['b_ln_b'][j], p['b_w_s'][j].astype(BF16), p['b_b_s'][j],
                         p['b_w_out'][j].astype(BF16), p['b_b_out'][j], g[0], b[0], alpha)
        else:
            w1, w2, lam = _s5_weights(p['c_a_re'][j], p['c_a_im'][j], p['c_b_re'][j], p['c_b_im'][j],
                                      p['c_c_re'][j], p['c_c_im'][j], p['c_log_dt'][j])
            y = _s5_core(x, bsz, seq, w1, w2, lam)
            x = _s5_out(y, x, p['c_d'][j], p['c_w_glu'][j].astype(BF16), p['c_b_glu'][j], g[0], b[0], alpha)
        wkv = jnp.concatenate([p['x_wk'][i], p['x_wv'][i]], axis=1).astype(BF16)
        kv = _mm(mem, wkv, BF16).reshape(bsz, -1, 2 * d)
        x = _xattn(x.reshape(bsz, seq, d), kv[:, :, :d], kv[:, :, d:], p['x_wq'][i].astype(BF16),
                   p['x_wo'][i].astype(BF16), g[1], b[1], N_XHEADS, alpha).reshape(m, d)
        x = _ec_moe(x, p['e_w_router'][i], p['e_w_gate'][i], p['e_w_up'][i], p['e_w_down'][i],
                    g[2], b[2], alpha, EC_CAPACITY)
    return x.reshape(bsz, seq, d)


def kernel(x_prompt, x_sample, mem_prompt, mem_sample, a_w_in, a_b_in, a_dw, a_dw_b, a_ln_g, a_ln_b, a_w_out, a_b_out, b_w_in, b_b_in, b_ln_g, b_ln_b, b_w_s, b_b_s, b_w_out, b_b_out, c_a_re, c_a_im, c_b_re, c_b_im, c_c_re, c_c_im, c_log_dt, c_d, c_w_glu, c_b_glu, x_wq, x_wk, x_wv, x_wo, e_w_router, e_w_gate, e_w_up, e_w_down, ln_g, ln_b):
    p = dict(a_w_in=a_w_in, a_b_in=a_b_in, a_dw=a_dw, a_dw_b=a_dw_b, a_ln_g=a_ln_g, a_ln_b=a_ln_b,
             a_w_out=a_w_out, a_b_out=a_b_out,
             b_w_in=b_w_in, b_b_in=b_b_in, b_ln_g=b_ln_g, b_ln_b=b_ln_b, b_w_s=b_w_s, b_b_s=b_b_s,
             b_w_out=b_w_out, b_b_out=b_b_out,
             c_a_re=c_a_re, c_a_im=c_a_im, c_b_re=c_b_re, c_b_im=c_b_im, c_c_re=c_c_re, c_c_im=c_c_im,
             c_log_dt=c_log_dt, c_d=c_d, c_w_glu=c_w_glu, c_b_glu=c_b_glu,
             x_wq=x_wq, x_wk=x_wk, x_wv=x_wv, x_wo=x_wo,
             e_w_router=e_w_router, e_w_gate=e_w_gate, e_w_up=e_w_up, e_w_down=e_w_down,
             ln_g=ln_g, ln_b=ln_b)
    return (_trunk(x_prompt, mem_prompt, p), _trunk(x_sample, mem_sample, p))
```

```python
import functools
import math

import jax
import jax.numpy as jnp
from jax import lax
from jax.experimental import pallas as pl
from jax.experimental.pallas import tpu as pltpu

F32 = jnp.float32
BF16 = jnp.bfloat16
LN_EPS = 1e-5
DT_HALO = 16
S5_CHUNK = 16
S5_LANES = 128
VMEM_LIMIT = 56 * 1024 * 1024


def _cparams(sem):
    return pltpu.CompilerParams(dimension_semantics=sem, vmem_limit_bytes=VMEM_LIMIT)


def _resident(shape):
    nd = len(shape)
    return pl.BlockSpec(shape, lambda *_: (0,) * nd, pipeline_mode=pl.Buffered(1))


def _ln(z, g, b):
    mu = jnp.mean(z, axis=-1, keepdims=True)
    zc = z - mu
    var = jnp.mean(zc * zc, axis=-1, keepdims=True)
    return zc * lax.rsqrt(var + LN_EPS) * g + b


def _row(v):
    return v.reshape(1, -1).astype(F32)


def _pick(n, pref):
    t = min(n, pref)
    while n % t:
        t //= 2
    return t


def _glu_in_kernel(x_ref, wa_ref, wg_ref, ba_ref, bg_ref, o_ref):
    xb = x_ref[...].astype(BF16)
    a = jnp.dot(xb, wa_ref[...], preferred_element_type=F32) + ba_ref[...]
    g = jnp.dot(xb, wg_ref[...], preferred_element_type=F32) + bg_ref[...]
    o_ref[...] = a * jax.nn.sigmoid(g)


def _glu_in(x, w, b):
    m, d = x.shape
    h = w.shape[1] // 2
    tm, tn = _pick(m, 1024), _pick(h, 512)
    nj = h // tn
    b = _row(b)
    return pl.pallas_call(
        _glu_in_kernel,
        grid=(m // tm, nj),
        in_specs=[pl.BlockSpec((tm, d), lambda i, j: (i, 0)),
                  pl.BlockSpec((d, tn), lambda i, j: (0, j)),
                  pl.BlockSpec((d, tn), lambda i, j: (0, j + nj)),
                  pl.BlockSpec((1, tn), lambda i, j: (0, j)),
                  pl.BlockSpec((1, tn), lambda i, j: (0, j + nj))],
        out_specs=pl.BlockSpec((tm, tn), lambda i, j: (i, j)),
        out_shape=jax.ShapeDtypeStruct((m, h), F32),
        compiler_params=_cparams(("parallel", "arbitrary")),
    )(x, w, w, b, b)


def _gelu_in_kernel(x_ref, w_ref, b_ref, o_ref):
    xb = x_ref[...].astype(BF16)
    o_ref[...] = jax.nn.gelu(jnp.dot(xb, w_ref[...], preferred_element_type=F32) + b_ref[...])


def _gelu_in(x, w, b):
    m, d = x.shape
    n = w.shape[1]
    tm, tn = _pick(m, 1024), _pick(n, 512)
    return pl.pallas_call(
        _gelu_in_kernel,
        grid=(m // tm, n // tn),
        in_specs=[pl.BlockSpec((tm, d), lambda i, j: (i, 0)),
                  pl.BlockSpec((d, tn), lambda i, j: (0, j)),
                  pl.BlockSpec((1, tn), lambda i, j: (0, j))],
        out_specs=pl.BlockSpec((tm, tn), lambda i, j: (i, j)),
        out_shape=jax.ShapeDtypeStruct((m, n), F32),
        compiler_params=_cparams(("parallel", "arbitrary")),
    )(x, w, _row(b))


def _mm_kernel(x_ref, w_ref, o_ref):
    o_ref[...] = jnp.dot(x_ref[...].astype(BF16), w_ref[...], preferred_element_type=F32).astype(o_ref.dtype)


def _mm(x, w, out_dtype):
    m, d = x.shape
    n = w.shape[1]
    tm, tn = _pick(m, 512), _pick(n, 512)
    return pl.pallas_call(
        _mm_kernel,
        grid=(m // tm, n // tn),
        in_specs=[pl.BlockSpec((tm, d), lambda i, j: (i, 0)),
                  pl.BlockSpec((d, tn), lambda i, j: (0, j))],
        out_specs=pl.BlockSpec((tm, tn), lambda i, j: (i, j)),
        out_shape=jax.ShapeDtypeStruct((m, n), out_dtype),
        compiler_params=_cparams(("parallel", "arbitrary")),
    )(x, w)


def _conv_out_kernel(hp_ref, hc_ref, hn_ref, x_ref, dw_ref, dwb_ref, ag_ref, ab_ref, w_ref, bo_ref,
                     g_ref, b_ref, o_ref, hext_ref, cv_ref, sh_ref, *, tiles_per_seq, alpha, taps):
    i = pl.program_id(0)
    ts, d = hc_ref.shape
    first = (i % tiles_per_seq) == 0
    last = (i % tiles_per_seq) == tiles_per_seq - 1
    hext_ref[0:DT_HALO, :] = jnp.where(first, 0.0, hp_ref[...])
    hext_ref[DT_HALO:DT_HALO + ts, :] = hc_ref[...]
    hext_ref[DT_HALO + ts:, :] = jnp.where(last, 0.0, hn_ref[...])
    base = DT_HALO - taps // 2
    sub = 8
    rblk = _pick(ts, 128)
    nq = -(-(base + taps) // sub)
    assert sh_ref.shape[0] == rblk + (nq - 1) * sub

    def col_block(c, carry):
        cols = pl.ds(pl.multiple_of(c * 128, 128), 128)
        for rb in range(ts // rblk):
            acc = jnp.zeros((rblk, 128), F32)
            for rho in range(sub):
                sh_ref[...] = hext_ref[pl.ds(rb * rblk + rho, rblk + (nq - 1) * sub), cols]
                for q in range(nq):
                    k = q * sub + rho - base
                    if 0 <= k < taps:
                        acc = acc + sh_ref[q * sub:q * sub + rblk, :] * dw_ref[pl.ds(k, 1), cols]
            cv_ref[rb * rblk:(rb + 1) * rblk, cols] = acc
        return carry

    lax.fori_loop(0, d // 128, col_block, 0)
    z = _ln(cv_ref[...] + dwb_ref[...], ag_ref[...], ab_ref[...])
    z = (z * jax.nn.sigmoid(z)).astype(BF16)
    t = jnp.dot(z, w_ref[...], preferred_element_type=F32) + bo_ref[...]
    o_ref[...] = _ln(alpha * x_ref[...] + t, g_ref[...], b_ref[...])


def _conv_out(h, x, seq, dw, dw_b, a_g, a_b, w_out, b_out, g, b, alpha):
    m, d = x.shape
    taps = dw.shape[0]
    assert taps // 2 <= DT_HALO
    ts = _pick(seq, 256)
    hb = ts // DT_HALO
    nhalo = m // DT_HALO
    kern = functools.partial(_conv_out_kernel, tiles_per_seq=seq // ts, alpha=alpha, taps=taps)
    return pl.pallas_call(
        kern,
        grid=(m // ts,),
        in_specs=[pl.BlockSpec((DT_HALO, d), lambda i: (jnp.maximum(i * hb - 1, 0), 0)),
                  pl.BlockSpec((ts, d), lambda i: (i, 0)),
                  pl.BlockSpec((DT_HALO, d), lambda i: (jnp.minimum((i + 1) * hb, nhalo - 1), 0)),
                  pl.BlockSpec((ts, d), lambda i: (i, 0)),
                  _resident(dw.shape), _resident((1, d)), _resident((1, d)), _resident((1, d)),
                  _resident(w_out.shape), _resident((1, d)), _resident((1, d)), _resident((1, d))],
        out_specs=pl.BlockSpec((ts, d), lambda i: (i, 0)),
        out_shape=jax.ShapeDtypeStruct((m, d), F32),
        scratch_shapes=[pltpu.VMEM((ts + 2 * DT_HALO, d), F32), pltpu.VMEM((ts, d), F32),
                        pltpu.VMEM((_pick(ts, 128) + 2 * DT_HALO - 8, 128), F32)],
        compiler_params=_cparams(("parallel",)),
    )(h, h, h, x, dw, _row(dw_b), _row(a_g), _row(a_b), w_out, _row(b_out), _row(g), _row(b))


def _sgu_out_kernel(u_ref, v_ref, x_ref, vg_ref, vb_ref, ws_ref, bs_ref, w_ref, bo_ref, g_ref, b_ref,
                    o_ref, z_ref, *, alpha):
    tm, d = u_ref.shape
    nh, chunk, _ = ws_ref.shape
    hw = d // nh
    vn = _ln(v_ref[...], vg_ref[...], vb_ref[...]).astype(BF16)
    for r in range(tm // chunk):
        rows = slice(r * chunk, (r + 1) * chunk)
        for hd in range(nh):
            cols = slice(hd * hw, (hd + 1) * hw)
            s = jnp.dot(ws_ref[hd], vn[rows, cols], preferred_element_type=F32) + bs_ref[hd]
            z_ref[rows, cols] = (u_ref[rows, cols] * s).astype(BF16)
    t = jnp.dot(z_ref[...], w_ref[...], preferred_element_type=F32) + bo_ref[...]
    o_ref[...] = _ln(alpha * x_ref[...] + t, g_ref[...], b_ref[...])


def _sgu_out(h, x, v_g, v_b, w_s, b_s, w_out, b_out, g, b, alpha):
    m, d = x.shape
    nh, chunk, _ = w_s.shape
    tm = 2 * chunk
    assert m % tm == 0
    bs3 = b_s.astype(F32)[:, :, None]
    return pl.pallas_call(
        functools.partial(_sgu_out_kernel, alpha=alpha),
        grid=(m // tm,),
        in_specs=[pl.BlockSpec((tm, d), lambda i: (i, 0)),
                  pl.BlockSpec((tm, d), lambda i: (i, 1)),
                  pl.BlockSpec((tm, d), lambda i: (i, 0)),
                  _resident((1, d)), _resident((1, d)), _resident(w_s.shape), _resident(bs3.shape),
                  _resident(w_out.shape), _resident((1, d)), _resident((1, d)), _resident((1, d))],
        out_specs=pl.BlockSpec((tm, d), lambda i: (i, 0)),
        out_shape=jax.ShapeDtypeStruct((m, d), F32),
        scratch_shapes=[pltpu.VMEM((tm, d), BF16)],
        compiler_params=_cparams(("parallel",)),
    )(h, h, x, _row(v_g), _row(v_b), w_s, bs3, w_out, _row(b_out), _row(g), _row(b))


def _s5_weights(a_re, a_im, b_re, b_im, c_re, c_im, log_dt):
    hp = lax.Precision.HIGHEST
    L = S5_CHUNK
    _, G, P = a_re.shape
    GW = b_re.shape[-1]
    a_re = a_re.astype(F32); a_im = a_im.astype(F32)
    dt = jnp.exp(log_dt.astype(F32))[..., None]
    n = jnp.arange(L + 1, dtype=F32)[:, None, None, None]
    mag = jnp.exp(n * (a_re * dt)[None])
    ang = n * (a_im * dt)[None]
    pw_re = mag * jnp.cos(ang)
    pw_im = mag * jnp.sin(ang)
    lb_re, lb_im = pw_re[1], pw_im[1]
    den = a_re * a_re + a_im * a_im
    n_re = lb_re - 1.0
    f_re = (n_re * a_re + lb_im * a_im) / den
    f_im = (lb_im * a_re - n_re * a_im) / den
    bb_re = f_re[..., None] * b_re - f_im[..., None] * b_im
    bb_im = f_re[..., None] * b_im + f_im[..., None] * b_re
    c_re = c_re.astype(F32); c_im = c_im.astype(F32)
    ct_re = c_re.transpose(0, 1, 3, 2)[..., None]
    ct_im = c_im.transpose(0, 1, 3, 2)[..., None]
    cb_re = ct_re * bb_re[..., None, :] - ct_im * bb_im[..., None, :]
    cb_im = ct_re * bb_im[..., None, :] + ct_im * bb_re[..., None, :]
    kern = (jnp.einsum('ndgp,dgpck->dgnck', pw_re[:L], cb_re, precision=hp)
            - jnp.einsum('ndgp,dgpck->dgnck', pw_im[:L], cb_im, precision=hp))
    s_i = jnp.arange(L)[:, None]
    t_i = jnp.arange(L)[None, :]
    lag_f = t_i - s_i
    lag_b = s_i - t_i
    tf = jnp.where((lag_f >= 0)[None, :, :, None, None], kern[0][:, jnp.clip(lag_f, 0, L - 1)], 0.0)
    tb = jnp.where((lag_b >= 0)[None, :, :, None, None], kern[1][:, jnp.clip(lag_b, 0, L - 1)], 0.0)
    toep = (tf + tb).transpose(0, 1, 4, 2, 3).reshape(G, L * GW, L * GW)
    ef = jnp.arange(L - 1, -1, -1)
    eb = jnp.arange(L)

    def inc(d, e):
        pr, pi = pw_re[e, d], pw_im[e, d]
        re = pr[..., None] * bb_re[d][None] - pi[..., None] * bb_im[d][None]
        im = pr[..., None] * bb_im[d][None] + pi[..., None] * bb_re[d][None]
        to = lambda z: z.transpose(1, 0, 3, 2).reshape(G, L * GW, P)
        return to(re), to(im)

    sf_re, sf_im = inc(0, ef)
    sb_re, sb_im = inc(1, eb)
    w1g = jnp.stack([sf_re, sf_im, sb_re, sb_im], axis=2)
    of = jnp.arange(1, L + 1)
    ob = jnp.arange(L, 0, -1)

    def outm(d, e):
        pr, pi = pw_re[e, d], pw_im[e, d]
        re = c_re[d][None] * pr[:, :, None, :] - c_im[d][None] * pi[:, :, None, :]
        im = c_re[d][None] * pi[:, :, None, :] + c_im[d][None] * pr[:, :, None, :]
        to = lambda z: z.transpose(1, 3, 0, 2).reshape(G, P, L * GW)
        return to(re), to(-im)

    mf_re, mf_im = outm(0, of)
    mb_re, mb_im = outm(1, ob)
    w2g = jnp.stack([mf_re, mf_im, mb_re, mb_im], axis=1)
    gl = S5_LANES // GW
    eye = jnp.eye(gl, dtype=BF16)
    q = G // gl
    w1 = jnp.einsum('jalcpk,ab->jlacpbk', w1g.astype(BF16).reshape(q, gl, L, GW, 4, P), eye)
    w1 = w1.reshape(q, L * gl * GW, 4 * gl * P)
    tp = jnp.einsum('jasktc,ab->jsaktbc', toep.astype(BF16).reshape(q, gl, L, GW, L, GW), eye)
    tp = tp.reshape(q, L * gl * GW, L * gl * GW)
    w2s = jnp.einsum('japktc,ab->jpaktbc', w2g.astype(BF16).reshape(q, gl, 4, P, L, GW), eye)
    w2s = w2s.reshape(q, 4 * gl * P, L * gl * GW)
    w2 = jnp.concatenate([tp, w2s], axis=1)
    lam_g = jnp.stack([pw_re[L, 0], pw_im[L, 0], pw_re[L, 1], pw_im[L, 1]], axis=1)
    lam = lam_g.reshape(q, gl, 4, P).transpose(0, 2, 1, 3).reshape(q, 4 * gl * P)
    return w1, w2, lam


def _s5_rows(x_ref):
    return jnp.concatenate([x_ref[:, l, :].astype(BF16) for l in range(x_ref.shape[1])], axis=1)


def _s5_inc_kernel(x_ref, w_ref, o_ref):
    o_ref[...] = jnp.dot(_s5_rows(x_ref), w_ref[0], preferred_element_type=F32)


def _s5_mix_kernel(x_ref, h_ref, w_ref, o_ref):
    _, L, lanes = x_ref.shape
    kx = L * lanes
    y = jnp.dot(_s5_rows(x_ref), w_ref[0, :kx, :], preferred_element_type=F32)
    y = y + jnp.dot(h_ref[...].astype(BF16), w_ref[0, kx:, :], preferred_element_type=F32)
    for t in range(L):
        o_ref[:, t, :] = y[:, t * lanes:(t + 1) * lanes]


def _s5_scan_kernel(s_ref, lam_ref, h_ref, *, pairs, lanes):
    nc = s_ref.shape[0]
    sub = 8
    pl_w = lanes // 4
    zero = jnp.zeros((1, pl_w), F32)

    def step(kk, carry):
        rf = pl.ds(pl.multiple_of(kk * sub, sub), sub)
        rb = pl.ds(pl.multiple_of(nc - sub - kk * sub, sub), sub)
        new = []
        for j in range(pairs):
            o = j * lanes
            cf_r, cf_i = slice(o, o + pl_w), slice(o + pl_w, o + 2 * pl_w)
            cb_r, cb_i = slice(o + 2 * pl_w, o + 3 * pl_w), slice(o + 3 * pl_w, o + 4 * pl_w)
            fr, fi, br, bi = carry[4 * j:4 * j + 4]
            lfr, lfi, lbr, lbi = lam_ref[:, cf_r], lam_ref[:, cf_i], lam_ref[:, cb_r], lam_ref[:, cb_i]
            sfr, sfi, sbr, sbi = s_ref[rf, cf_r], s_ref[rf, cf_i], s_ref[rb, cb_r], s_ref[rb, cb_i]
            ofr, ofi, obr, obi = [], [], [], []
            for r in range(sub):
                ofr.append(fr)
                ofi.append(fi)
                fr, fi = lfr * fr - lfi * fi + sfr[r:r + 1], lfr * fi + lfi * fr + sfi[r:r + 1]
            for r in range(sub - 1, -1, -1):
                obr.append(br)
                obi.append(bi)
                br, bi = lbr * br - lbi * bi + sbr[r:r + 1], lbr * bi + lbi * br + sbi[r:r + 1]
            h_ref[rf, cf_r] = jnp.concatenate(ofr, axis=0)
            h_ref[rf, cf_i] = jnp.concatenate(ofi, axis=0)
            h_ref[rb, cb_r] = jnp.concatenate(obr[::-1], axis=0)
            h_ref[rb, cb_i] = jnp.concatenate(obi[::-1], axis=0)
            new += [fr, fi, br, bi]
        return tuple(new)

    lax.fori_loop(0, nc // sub, step, (zero,) * (4 * pairs))


def _s5_core(x, bsz, seq, w1, w2, lam):
    m, d = x.shape
    L = S5_CHUNK
    q, kx, ns = w1.shape
    lanes = kx // L
    nch = m // L
    nc = seq // L
    x3 = x.reshape(nch, L, d)
    tm = _pick(nch, 256)
    s = pl.pallas_call(
        _s5_inc_kernel,
        grid=(q, nch // tm),
        in_specs=[pl.BlockSpec((tm, L, lanes), lambda p, i: (i, 0, p)),
                  pl.BlockSpec((1, kx, ns), lambda p, i: (p, 0, 0))],
        out_specs=pl.BlockSpec((tm, ns), lambda p, i: (i, p)),
        out_shape=jax.ShapeDtypeStruct((nch, q * ns), F32),
        compiler_params=_cparams(("parallel", "arbitrary")),
    )(x3, w1)
    h = pl.pallas_call(
        functools.partial(_s5_scan_kernel, pairs=1, lanes=ns),
        grid=(bsz, q),
        in_specs=[pl.BlockSpec((nc, ns), lambda b, p: (b, p)),
                  pl.BlockSpec((1, ns), lambda b, p: (0, p))],
        out_specs=pl.BlockSpec((nc, ns), lambda b, p: (b, p)),
        out_shape=jax.ShapeDtypeStruct((nch, q * ns), F32),
        compiler_params=_cparams(("parallel", "arbitrary")),
    )(s, lam.reshape(1, q * ns))
    y3 = pl.pallas_call(
        _s5_mix_kernel,
        grid=(q, nch // tm),
        in_specs=[pl.BlockSpec((tm, L, lanes), lambda p, i: (i, 0, p)),
                  pl.BlockSpec((tm, ns), lambda p, i: (i, p)),
                  pl.BlockSpec((1, kx + ns, kx), lambda p, i: (p, 0, 0))],
        out_specs=pl.BlockSpec((tm, L, lanes), lambda p, i: (i, 0, p)),
        out_shape=jax.ShapeDtypeStruct((nch, L, d), F32),
        compiler_params=_cparams(("parallel", "arbitrary")),
    )(x3, h, w2)
    return y3.reshape(m, d)


def _s5_out_kernel(y_ref, x_ref, d_ref, wv_ref, wg_ref, bv_ref, bg_ref, g_ref, b_ref, o_ref, z_ref, t_ref,
                   *, alpha, nj):
    j = pl.program_id(1)
    tn = wv_ref.shape[1]

    @pl.when(j == 0)
    def _():
        z_ref[...] = jax.nn.gelu(y_ref[...] + d_ref[...] * x_ref[...]).astype(BF16)

    zb = z_ref[...]
    val = jnp.dot(zb, wv_ref[...], preferred_element_type=F32) + bv_ref[...]
    gate = jnp.dot(zb, wg_ref[...], preferred_element_type=F32) + bg_ref[...]
    t_ref[:, pl.ds(pl.multiple_of(j * tn, tn), tn)] = val * jax.nn.sigmoid(gate)

    @pl.when(j == nj - 1)
    def _():
        o_ref[...] = _ln(alpha * x_ref[...] + t_ref[...], g_ref[...], b_ref[...])


def _s5_out(y, x, dskip, w_glu, b_glu, g, b, alpha):
    m, d = x.shape
    tm, tn = _pick(m, 512), _pick(d, 512)
    nj = d // tn
    bg = _row(b_glu)
    return pl.pallas_call(
        functools.partial(_s5_out_kernel, alpha=alpha, nj=nj),
        grid=(m // tm, nj),
        in_specs=[pl.BlockSpec((tm, d), lambda i, j: (i, 0)),
                  pl.BlockSpec((tm, d), lambda i, j: (i, 0)),
                  pl.BlockSpec((1, d), lambda i, j: (0, 0)),
                  pl.BlockSpec((d, tn), lambda i, j: (0, j)),
                  pl.BlockSpec((d, tn), lambda i, j: (0, j + nj)),
                  pl.BlockSpec((1, tn), lambda i, j: (0, j)),
                  pl.BlockSpec((1, tn), lambda i, j: (0, j + nj)),
                  pl.BlockSpec((1, d), lambda i, j: (0, 0)),
                  pl.BlockSpec((1, d), lambda i, j: (0, 0))],
        out_specs=pl.BlockSpec((tm, d), lambda i, j: (i, 0)),
        out_shape=jax.ShapeDtypeStruct((m, d), F32),
        scratch_shapes=[pltpu.VMEM((tm, d), BF16), pltpu.VMEM((tm, d), F32)],
        compiler_params=_cparams(("parallel", "arbitrary")),
    )(y, x, _row(dskip), w_glu, w_glu, bg, bg, _row(g), _row(b))


def _xattn_kernel(x_ref, k_ref, v_ref, wq_ref, wo_ref, g_ref, b_ref, wr_ref, o_ref, p_ref, a_ref,
                  *, heads, alpha, n_exp):
    x = x_ref[0]
    d = x.shape[1]
    dh = d // heads
    scale = dh ** -0.5
    xb = x.astype(BF16)
    for hd in range(heads):
        cols = slice(hd * dh, (hd + 1) * dh)
        qh = jnp.dot(xb, wq_ref[:, cols], preferred_element_type=F32).astype(BF16)
        s = lax.dot_general(qh, k_ref[0, :, cols], (((1,), (1,)), ((), ())), preferred_element_type=F32) * scale
        s = s - jnp.max(s, axis=-1, keepdims=True)
        e = jnp.exp(s)
        p = (e / jnp.sum(e, axis=-1, keepdims=True)).astype(BF16)
        a_ref[:, cols] = jnp.dot(p, v_ref[0, :, cols], preferred_element_type=F32).astype(BF16)
    t = jnp.dot(a_ref[...], wo_ref[...], preferred_element_type=F32)
    y = _ln(alpha * x + t, g_ref[...], b_ref[...])
    o_ref[0] = y
    logits = jnp.dot(y, wr_ref[...], preferred_element_type=F32, precision=lax.Precision.HIGHEST)
    lane = lax.broadcasted_iota(jnp.int32, logits.shape, 1)
    logits = jnp.where(lane < n_exp, logits, -jnp.inf)
    e = jnp.exp(logits - jnp.max(logits, axis=-1, keepdims=True))
    p = e / jnp.sum(e, axis=-1, keepdims=True)
    p_ref[...] = p.T[:p_ref.shape[0], :]


def _xattn(x3, k3, v3, wq, wo, g, b, w_router, heads, alpha):
    bsz, seq, d = x3.shape
    nm = k3.shape[1]
    n_exp = w_router.shape[1]
    assert n_exp <= 128
    erows = -(-n_exp // 8) * 8
    wpad = jnp.zeros((d, 128), F32).at[:, :n_exp].set(w_router.astype(F32))
    tm = _pick(seq, 512)
    nt = seq // tm
    y, probs_t = pl.pallas_call(
        functools.partial(_xattn_kernel, heads=heads, alpha=alpha, n_exp=n_exp),
        grid=(bsz, nt),
        in_specs=[pl.BlockSpec((1, tm, d), lambda bi, i: (bi, i, 0)),
                  pl.BlockSpec((1, nm, d), lambda bi, i: (bi, 0, 0)),
                  pl.BlockSpec((1, nm, d), lambda bi, i: (bi, 0, 0)),
                  _resident(wq.shape), _resident(wo.shape), _resident((1, d)), _resident((1, d)),
                  _resident((d, 128))],
        out_specs=[pl.BlockSpec((1, tm, d), lambda bi, i: (bi, i, 0)),
                   pl.BlockSpec((erows, tm), lambda bi, i: (0, bi * nt + i))],
        out_shape=[jax.ShapeDtypeStruct((bsz, seq, d), F32),
                   jax.ShapeDtypeStruct((erows, bsz * seq), F32)],
        scratch_shapes=[pltpu.VMEM((tm, d), BF16)],
        compiler_params=_cparams(("parallel", "arbitrary")),
    )(x3, k3, v3, wq, wo, _row(g), _row(b), wpad)
    return y, probs_t[:n_exp]


def _cumsum_tokens(mask, upper, lstrict):
    local = jnp.dot(mask.astype(BF16), upper, preferred_element_type=F32)
    tot = jnp.sum(mask, axis=1, keepdims=True)
    off = jnp.dot(lstrict, jnp.broadcast_to(tot, mask.shape).astype(BF16), preferred_element_type=F32)
    return off + local, tot


def _ec_select_kernel(p_ref, idx_ref, gate_ref, pos_ref, start_ref, end_ref, *, cap, sc):
    p = p_ref[0]
    R = p.shape[0]
    bits = pltpu.bitcast(p, jnp.int32)

    def search(i, thr):
        cand = thr | jnp.left_shift(jnp.int32(1), 30 - i)
        cnt = jnp.sum((bits >= cand).astype(F32), axis=(0, 1), keepdims=True)
        return jnp.where(cnt >= cap, cand, thr)

    thr = lax.fori_loop(0, 31, search, jnp.zeros((1, 1), jnp.int32))
    ii = lax.broadcasted_iota(jnp.int32, (128, 128), 0)
    jj = lax.broadcasted_iota(jnp.int32, (128, 128), 1)
    upper = (ii <= jj).astype(BF16)
    ri = lax.broadcasted_iota(jnp.int32, (R, R), 0)
    rj = lax.broadcasted_iota(jnp.int32, (R, R), 1)
    lstrict = (rj < ri).astype(BF16)
    uincl = (ri <= rj).astype(BF16)

    gt = (bits > thr).astype(F32)
    eq = (bits == thr).astype(F32)
    need = cap - jnp.sum(gt, axis=(0, 1), keepdims=True)
    tie_incl, _ = _cumsum_tokens(eq, upper, lstrict)
    sel = gt + eq * ((tie_incl - eq) < need).astype(F32)
    cs, _ = _cumsum_tokens(sel, upper, lstrict)
    pos_ref[0] = jnp.where(sel > 0.5, cs - 1.0, -1.0).astype(jnp.int32)

    ones8 = jnp.ones((8, 128), BF16)
    tot_row = lax.dot_general(ones8, sel.astype(BF16), (((1,), (1,)), ((), ())), preferred_element_type=F32)
    end_row = jnp.dot(tot_row.astype(BF16), uincl, preferred_element_type=F32)[0:1]
    start_row = end_row - tot_row[0:1]
    start_ref[0] = start_row.astype(jnp.int32)
    end_ref[0] = end_row.astype(jnp.int32)

    cs_hi = jnp.floor(cs * (1.0 / 64.0))
    cs_lo = cs - 64.0 * cs_hi
    p1 = p.astype(BF16)
    p2 = (p - p1.astype(F32)).astype(BF16)
    p3 = (p - p1.astype(F32) - p2.astype(F32)).astype(BF16)
    rhs = jnp.concatenate([cs_hi.astype(BF16), cs_lo.astype(BF16), sel.astype(BF16), p1, p2, p3], axis=1)
    for c in range(cap // sc):
        s_col = (lax.broadcasted_iota(jnp.int32, (sc, 1), 0) + c * sc).astype(F32)
        onehot_row = jnp.logical_and(start_row <= s_col, s_col < end_row).astype(BF16)
        g = jnp.dot(onehot_row, rhs, preferred_element_type=F32)
        csg = 64.0 * g[:, 0:128] + g[:, 128:256]
        tok_in_row = jnp.sum((csg <= s_col).astype(F32), axis=1, keepdims=True)
        row = jnp.sum((end_row <= s_col).astype(F32), axis=1, keepdims=True)
        hit = jnp.logical_and(csg == s_col + 1.0, g[:, 256:384] > 0.5)
        pg = (g[:, 640:768] + g[:, 512:640]) + g[:, 384:512]
        idx_ref[0, c * sc:(c + 1) * sc, :] = (128.0 * row + tok_in_row).astype(jnp.int32)
        gate_ref[0, c * sc:(c + 1) * sc, :] = jnp.sum(jnp.where(hit, pg, 0.0), axis=1, keepdims=True)


def _ec_select(probs_t, cap):
    n_exp, n = probs_t.shape
    assert n % 128 == 0 and cap <= 64 * 256
    r = n // 128
    sc = _pick(cap, 512)
    idx, gate, pos, start, end = pl.pallas_call(
        functools.partial(_ec_select_kernel, cap=cap, sc=sc),
        grid=(n_exp,),
        in_specs=[pl.BlockSpec((1, r, 128), lambda e: (e, 0, 0))],
        out_specs=[pl.BlockSpec((1, cap, 1), lambda e: (e, 0, 0)),
                   pl.BlockSpec((1, cap, 1), lambda e: (e, 0, 0)),
                   pl.BlockSpec((1, r, 128), lambda e: (e, 0, 0)),
                   pl.BlockSpec((1, 1, r), lambda e: (e, 0, 0)),
                   pl.BlockSpec((1, 1, r), lambda e: (e, 0, 0))],
        out_shape=[jax.ShapeDtypeStruct((n_exp, cap, 1), jnp.int32),
                   jax.ShapeDtypeStruct((n_exp, cap, 1), F32),
                   jax.ShapeDtypeStruct((n_exp, r, 128), jnp.int32),
                   jax.ShapeDtypeStruct((n_exp, 1, r), jnp.int32),
                   jax.ShapeDtypeStruct((n_exp, 1, r), jnp.int32)],
        compiler_params=_cparams(("parallel",)),
    )(probs_t.reshape(n_exp, r, 128))
    return idx.reshape(n_exp, cap), gate.reshape(n_exp, cap), pos, start.reshape(n_exp, r), end.reshape(n_exp, r)


def _moe_ffn_kernel(idx_ref, x_hbm, gate_ref, wg_ref, wu_ref, wd_ref, o_ref, xg_ref, xb_ref, sem, *, nf, ntiles):
    f = pl.program_id(2)
    tile = pl.program_id(0) * pl.num_programs(1) + pl.program_id(1)
    tm = xg_ref.shape[0]
    per = tm // nf

    def row_copy(slot, dst_row):
        return pltpu.make_async_copy(x_hbm.at[pl.ds(idx_ref[slot], 1)], xg_ref.at[pl.ds(dst_row, 1)], sem)

    @pl.when(jnp.logical_and(tile == 0, f == 0))
    def _():
        def issue(i, c):
            row_copy(i, i).start()
            return c

        lax.fori_loop(0, tm, issue, 0)

    @pl.when(f == 0)
    def _():
        pltpu.make_async_copy(x_hbm.at[pl.ds(0, tm)], xg_ref, sem).wait()
        xb_ref[...] = xg_ref[...].astype(BF16)
        o_ref[...] = jnp.zeros_like(o_ref)

    nxt = jnp.minimum(tile + 1, ntiles - 1) * tm + f * per
    for i in range(per):
        row_copy(nxt + i, f * per + i).start()

    xb = xb_ref[...]
    hg = jnp.dot(xb, wg_ref[0].astype(BF16), preferred_element_type=F32)
    hu = jnp.dot(xb, wu_ref[0].astype(BF16), preferred_element_type=F32)
    hh = (hg * jax.nn.sigmoid(hg) * hu).astype(BF16)
    o_ref[...] += jnp.dot(hh, wd_ref[0].astype(BF16), preferred_element_type=F32)

    @pl.when(f == nf - 1)
    def _():
        o_ref[...] = o_ref[...] * gate_ref[...]

    @pl.when(jnp.logical_and(tile == ntiles - 1, f == nf - 1))
    def _():
        pltpu.make_async_copy(x_hbm.at[pl.ds(0, tm)], xg_ref, sem).wait()


def _moe_ffn(x, idx, gate, w_gate, w_up, w_down):
    n, d = x.shape
    n_exp, cap = idx.shape
    dff = w_gate.shape[2]
    tm, tf = _pick(cap, 1024), _pick(dff, 256)
    nr, nf = cap // tm, dff // tf
    grid_spec = pltpu.PrefetchScalarGridSpec(
        num_scalar_prefetch=1,
        grid=(n_exp, nr, nf),
        in_specs=[pl.BlockSpec(memory_space=pl.ANY),
                  pl.BlockSpec((tm, 1), lambda e, r, f, idx: (e * nr + r, 0)),
                  pl.BlockSpec((1, d, tf), lambda e, r, f, idx: (e, 0, f)),
                  pl.BlockSpec((1, d, tf), lambda e, r, f, idx: (e, 0, f)),
                  pl.BlockSpec((1, tf, d), lambda e, r, f, idx: (e, f, 0))],
        out_specs=pl.BlockSpec((tm, d), lambda e, r, f, idx: (e * nr + r, 0)),
        scratch_shapes=[pltpu.VMEM((tm, d), F32), pltpu.VMEM((tm, d), BF16), pltpu.SemaphoreType.DMA(())],
    )
    return pl.pallas_call(
        functools.partial(_moe_ffn_kernel, nf=nf, ntiles=n_exp * nr),
        grid_spec=grid_spec,
        out_shape=jax.ShapeDtypeStruct((n_exp * cap, d), F32),
        compiler_params=_cparams(("arbitrary", "arbitrary", "arbitrary")),
    )(idx.reshape(-1).astype(jnp.int32), x, gate.reshape(-1, 1).astype(F32), w_gate, w_up, w_down)


COMBINE_PIECE = 8
COMBINE_K = 512


def _combine_kernel(first_ref, npiece_ref, ye_hbm, pos_ref, x_ref, g_ref, b_ref, o_ref, stage_ref, acc_ref, sem,
                    *, n_exp, cap, alpha):
    r = pl.program_id(0)
    tt = x_ref.shape[0]

    @pl.when(r == 0)
    def _():
        stage_ref[...] = jnp.zeros_like(stage_ref)

    base = jnp.int32(0)
    kpos = []
    for e in range(n_exp):
        first = first_ref[r * n_exp + e]
        npiece = npiece_ref[r * n_exp + e]

        def issue(p, c, first=first, base=base, e=e):
            src = ye_hbm.at[pl.ds(pl.multiple_of(e * cap + first + p * COMBINE_PIECE, COMBINE_PIECE), COMBINE_PIECE)]
            dst = stage_ref.at[pl.ds(pl.multiple_of(base + p * COMBINE_PIECE, COMBINE_PIECE), COMBINE_PIECE)]
            pltpu.make_async_copy(src, dst, sem).start()
            return c

        lax.fori_loop(0, npiece, issue, 0)
        pe = pos_ref[0, e:e + 1, :]
        kpos.append(jnp.where(pe >= 0, pe - first + base, -1))
        base = base + npiece * COMBINE_PIECE
    total = base

    def drain(p, c):
        pltpu.make_async_copy(ye_hbm.at[pl.ds(0, COMBINE_PIECE)], stage_ref.at[pl.ds(0, COMBINE_PIECE)], sem).wait()
        return c

    lax.fori_loop(0, total // COMBINE_PIECE, drain, 0)
    acc_ref[...] = jnp.zeros_like(acc_ref)

    def chunk(c, carry):
        k0 = pl.multiple_of(c * COMBINE_K, COMBINE_K)
        kio = lax.broadcasted_iota(jnp.int32, (COMBINE_K, tt), 0) + k0
        hit = kio == kpos[0]
        for e in range(1, n_exp):
            hit = jnp.logical_or(hit, kio == kpos[e])
        onehot = jnp.where(hit, 1.0, 0.0).astype(BF16)
        st = stage_ref[pl.ds(k0, COMBINE_K), :]
        hi = st.astype(BF16)
        lo = (st - hi.astype(F32)).astype(BF16)
        dn = (((0,), (0,)), ((), ()))
        acc_ref[...] += (lax.dot_general(onehot, hi, dn, preferred_element_type=F32)
                         + lax.dot_general(onehot, lo, dn, preferred_element_type=F32))
        return carry

    lax.fori_loop(0, (total + COMBINE_K - 1) // COMBINE_K, chunk, 0)
    o_ref[...] = _ln(alpha * x_ref[...] + acc_ref[...], g_ref[...], b_ref[...])


def _combine(ye, pos, start, end, x, g, b, alpha):
    n, d = x.shape
    n_exp, r, tt = pos.shape
    cap = ye.shape[0] // n_exp
    assert cap % COMBINE_PIECE == 0
    first = (start // COMBINE_PIECE) * COMBINE_PIECE
    npiece = jnp.where(end > start, (end - first + COMBINE_PIECE - 1) // COMBINE_PIECE, 0)
    stage_rows = n_exp * (tt + COMBINE_PIECE)
    stage_rows = -(-stage_rows // COMBINE_K) * COMBINE_K
    grid_spec = pltpu.PrefetchScalarGridSpec(
        num_scalar_prefetch=2,
        grid=(r,),
        in_specs=[pl.BlockSpec(memory_space=pl.ANY),
                  pl.BlockSpec((1, n_exp, tt), lambda i, a, c: (i, 0, 0)),
                  pl.BlockSpec((tt, d), lambda i, a, c: (i, 0)),
                  pl.BlockSpec((1, d), lambda i, a, c: (0, 0)),
                  pl.BlockSpec((1, d), lambda i, a, c: (0, 0))],
        out_specs=pl.BlockSpec((tt, d), lambda i, a, c: (i, 0)),
        scratch_shapes=[pltpu.VMEM((stage_rows, d), F32), pltpu.VMEM((tt, d), F32), pltpu.SemaphoreType.DMA(())],
    )
    return pl.pallas_call(
        functools.partial(_combine_kernel, n_exp=n_exp, cap=cap, alpha=alpha),
        grid_spec=grid_spec,
        out_shape=jax.ShapeDtypeStruct((n, d), F32),
        compiler_params=_cparams(("arbitrary",)),
    )(first.T.reshape(-1).astype(jnp.int32), npiece.T.reshape(-1).astype(jnp.int32), ye,
      pos.transpose(1, 0, 2), x, _row(g), _row(b))


def _ec_moe(x, probs_t, w_gate, w_up, w_down, g, b, alpha, ec_capacity):
    n, d = x.shape
    n_exp = probs_t.shape[0]
    cap = ec_capacity * n // n_exp
    idx, gate, pos, start, end = _ec_select(probs_t, cap)
    ye = _moe_ffn(x, idx, gate, w_gate, w_up, w_down)
    return _combine(ye, pos, start, end, x, g, b, alpha)


EC_CAPACITY = 2
N_XHEADS = 4
N_MIXERS = 3


def _trunk(x3, mem3, p):
    bsz, seq, d = x3.shape
    m = bsz * seq
    depth = p['ln_g'].shape[0]
    alpha = (2.0 * depth) ** 0.25
    x = x3.reshape(m, d)
    mem = mem3.reshape(-1, d)
    for i in range(depth):
        mix, j = i % N_MIXERS, i // N_MIXERS
        g, b = p['ln_g'][i], p['ln_b'][i]
        if mix == 0:
            h = _glu_in(x, p['a_w_in'][j].astype(BF16), p['a_b_in'][j])
            x = _conv_out(h, x, seq, p['a_dw'][j].astype(F32), p['a_dw_b'][j], p['a_ln_g'][j], p['a_ln_b'][j],
                          p['a_w_out'][j].astype(BF16), p['a_b_out'][j], g[0], b[0], alpha)
        elif mix == 1:
            h = _gelu_in(x, p['b_w_in'][j].astype(BF16), p['b_b_in'][j])
            x = _sgu_out(h, x, p['b_ln_g'][j], p['b_ln_b'][j], p['b_w_s'][j].astype(BF16), p['b_b_s'][j],
                         p['b_w_out'][j].astype(BF16), p['b_b_out'][j], g[0], b[0], alpha)
        else:
            w1, w2, lam = _s5_weights(p['c_a_re'][j], p['c_a_im'][j], p['c_b_re'][j], p['c_b_im'][j],
                                      p['c_c_re'][j], p['c_c_im'][j], p['c_log_dt'][j])
            y = _s5_core(x, bsz, seq, w1, w2, lam)
            x = _s5_out(y, x, p['c_d'][j], p['c_w_glu'][j].astype(BF16), p['c_b_glu'][j], g[0], b[0], alpha)
        wkv = jnp.concatenate([p['x_wk'][i], p['x_wv'][i]], axis=1).astype(BF16)
        kv = _mm(mem, wkv, BF16).reshape(bsz, -1, 2 * d)
        x, probs_t = _xattn(x.reshape(bsz, seq, d), kv[:, :, :d], kv[:, :, d:], p['x_wq'][i].astype(BF16),
                            p['x_wo'][i].astype(BF16), g[1], b[1], p['e_w_router'][i], N_XHEADS, alpha)
        x = _ec_moe(x.reshape(m, d), probs_t, p['e_w_gate'][i], p['e_w_up'][i], p['e_w_down'][i],
                    g[2], b[2], alpha, EC_CAPACITY)
    return x.reshape(bsz, seq, d)


def kernel(x_prompt, x_sample, mem_prompt, mem_sample, a_w_in, a_b_in, a_dw, a_dw_b, a_ln_g, a_ln_b, a_w_out, a_b_out, b_w_in, b_b_in, b_ln_g, b_ln_b, b_w_s, b_b_s, b_w_out, b_b_out, c_a_re, c_a_im, c_b_re, c_b_im, c_c_re, c_c_im, c_log_dt, c_d, c_w_glu, c_b_glu, x_wq, x_wk, x_wv, x_wo, e_w_router, e_w_gate, e_w_up, e_w_down, ln_g, ln_b):
    p = dict(a_w_in=a_w_in, a_b_in=a_b_in, a_dw=a_dw, a_dw_b=a_dw_b, a_ln_g=a_ln_g, a_ln_b=a_ln_b,
             a_w_out=a_w_out, a_b_out=a_b_out,
             b_w_in=b_w_in, b_b_in=b_b_in, b_ln_g=b_ln_g, b_ln_b=b_ln_b, b_w_s=b_w_s, b_b_s=b_b_s,
             b_w_out=b_w_out, b_b_out=b_b_out,
             c_a_re=c_a_re, c_a_im=c_a_im, c_b_re=c_b_re, c_b_im=c_b_im, c_c_re=c_c_re, c_c_im=c_c_im,
             c_log_dt=c_log_dt, c_d=c_d, c_w_glu=c_w_glu, c_b_glu=c_b_glu,
             x_wq=x_wq, x_wk=x_wk, x_wv=x_wv, x_wo=x_wo,
             e_w_router=e_w_router, e_w_gate=e_w_gate, e_w_up=e_w_up, e_w_down=e_w_down,
             ln_g=ln_g, ln_b=ln_b)
    return (_trunk(x_prompt, mem_prompt, p), _trunk(x_sample, mem_sample, p))
```

```python
import functools
import math

import jax
import jax.numpy as jnp
from jax import lax
from jax.experimental import pallas as pl
from jax.experimental.pallas import tpu as pltpu

F32 = jnp.float32
BF16 = jnp.bfloat16
LN_EPS = 1e-5
DT_HALO = 16
S5_CHUNK = 16
S5_LANES = 128
VMEM_LIMIT = 56 * 1024 * 1024


def _cparams(sem):
    return pltpu.CompilerParams(dimension_semantics=sem, vmem_limit_bytes=VMEM_LIMIT)


def _resident(shape):
    nd = len(shape)
    return pl.BlockSpec(shape, lambda *_: (0,) * nd, pipeline_mode=pl.Buffered(1))


def _ln(z, g, b):
    mu = jnp.mean(z, axis=-1, keepdims=True)
    zc = z - mu
    var = jnp.mean(zc * zc, axis=-1, keepdims=True)
    return zc * lax.rsqrt(var + LN_EPS) * g + b


def _row(v):
    return v.reshape(1, -1).astype(F32)


def _pick(n, pref):
    t = min(n, pref)
    while n % t:
        t //= 2
    return t


def _glu_in_kernel(x_ref, wa_ref, wg_ref, ba_ref, bg_ref, o_ref):
    xb = x_ref[...].astype(BF16)
    a = jnp.dot(xb, wa_ref[...], preferred_element_type=F32) + ba_ref[...]
    g = jnp.dot(xb, wg_ref[...], preferred_element_type=F32) + bg_ref[...]
    o_ref[...] = a * jax.nn.sigmoid(g)


def _glu_in(x, w, b):
    m, d = x.shape
    h = w.shape[1] // 2
    tm, tn = _pick(m, 1024), _pick(h, 512)
    nj = h // tn
    b = _row(b)
    return pl.pallas_call(
        _glu_in_kernel,
        grid=(m // tm, nj),
        in_specs=[pl.BlockSpec((tm, d), lambda i, j: (i, 0)),
                  pl.BlockSpec((d, tn), lambda i, j: (0, j)),
                  pl.BlockSpec((d, tn), lambda i, j: (0, j + nj)),
                  pl.BlockSpec((1, tn), lambda i, j: (0, j)),
                  pl.BlockSpec((1, tn), lambda i, j: (0, j + nj))],
        out_specs=pl.BlockSpec((tm, tn), lambda i, j: (i, j)),
        out_shape=jax.ShapeDtypeStruct((m, h), F32),
        compiler_params=_cparams(("parallel", "arbitrary")),
    )(x, w, w, b, b)


def _gelu_in_kernel(x_ref, w_ref, b_ref, o_ref):
    xb = x_ref[...].astype(BF16)
    o_ref[...] = jax.nn.gelu(jnp.dot(xb, w_ref[...], preferred_element_type=F32) + b_ref[...])


def _gelu_in(x, w, b):
    m, d = x.shape
    n = w.shape[1]
    tm, tn = _pick(m, 1024), _pick(n, 512)
    return pl.pallas_call(
        _gelu_in_kernel,
        grid=(m // tm, n // tn),
        in_specs=[pl.BlockSpec((tm, d), lambda i, j: (i, 0)),
                  pl.BlockSpec((d, tn), lambda i, j: (0, j)),
                  pl.BlockSpec((1, tn), lambda i, j: (0, j))],
        out_specs=pl.BlockSpec((tm, tn), lambda i, j: (i, j)),
        out_shape=jax.ShapeDtypeStruct((m, n), F32),
        compiler_params=_cparams(("parallel", "arbitrary")),
    )(x, w, _row(b))


def _mm_kernel(x_ref, w_ref, o_ref):
    o_ref[...] = jnp.dot(x_ref[...].astype(BF16), w_ref[...], preferred_element_type=F32).astype(o_ref.dtype)


def _mm(x, w, out_dtype):
    m, d = x.shape
    n = w.shape[1]
    tm, tn = _pick(m, 512), _pick(n, 512)
    return pl.pallas_call(
        _mm_kernel,
        grid=(m // tm, n // tn),
        in_specs=[pl.BlockSpec((tm, d), lambda i, j: (i, 0)),
                  pl.BlockSpec((d, tn), lambda i, j: (0, j))],
        out_specs=pl.BlockSpec((tm, tn), lambda i, j: (i, j)),
        out_shape=jax.ShapeDtypeStruct((m, n), out_dtype),
        compiler_params=_cparams(("parallel", "arbitrary")),
    )(x, w)


def _conv_out_kernel(hp_ref, hc_ref, hn_ref, x_ref, dw_ref, dwb_ref, ag_ref, ab_ref, w_ref, bo_ref,
                     g_ref, b_ref, o_ref, hext_ref, cv_ref, sh_ref, *, tiles_per_seq, alpha, taps):
    i = pl.program_id(0)
    ts, d = hc_ref.shape
    first = (i % tiles_per_seq) == 0
    last = (i % tiles_per_seq) == tiles_per_seq - 1
    hext_ref[0:DT_HALO, :] = jnp.where(first, 0.0, hp_ref[...])
    hext_ref[DT_HALO:DT_HALO + ts, :] = hc_ref[...]
    hext_ref[DT_HALO + ts:, :] = jnp.where(last, 0.0, hn_ref[...])
    base = DT_HALO - taps // 2
    sub = 8
    rblk = _pick(ts, 128)
    nq = -(-(base + taps) // sub)
    assert sh_ref.shape[0] == rblk + (nq - 1) * sub

    def col_block(c, carry):
        cols = pl.ds(pl.multiple_of(c * 128, 128), 128)
        for rb in range(ts // rblk):
            acc = jnp.zeros((rblk, 128), F32)
            for rho in range(sub):
                sh_ref[...] = hext_ref[pl.ds(rb * rblk + rho, rblk + (nq - 1) * sub), cols]
                for q in range(nq):
                    k = q * sub + rho - base
                    if 0 <= k < taps:
                        acc = acc + sh_ref[q * sub:q * sub + rblk, :] * dw_ref[pl.ds(k, 1), cols]
            cv_ref[rb * rblk:(rb + 1) * rblk, cols] = acc
        return carry

    lax.fori_loop(0, d // 128, col_block, 0)
    z = _ln(cv_ref[...] + dwb_ref[...], ag_ref[...], ab_ref[...])
    z = (z * jax.nn.sigmoid(z)).astype(BF16)
    t = jnp.dot(z, w_ref[...], preferred_element_type=F32) + bo_ref[...]
    o_ref[...] = _ln(alpha * x_ref[...] + t, g_ref[...], b_ref[...])


def _conv_out(h, x, seq, dw, dw_b, a_g, a_b, w_out, b_out, g, b, alpha):
    m, d = x.shape
    taps = dw.shape[0]
    assert taps // 2 <= DT_HALO
    ts = _pick(seq, 256)
    hb = ts // DT_HALO
    nhalo = m // DT_HALO
    kern = functools.partial(_conv_out_kernel, tiles_per_seq=seq // ts, alpha=alpha, taps=taps)
    return pl.pallas_call(
        kern,
        grid=(m // ts,),
        in_specs=[pl.BlockSpec((DT_HALO, d), lambda i: (jnp.maximum(i * hb - 1, 0), 0)),
                  pl.BlockSpec((ts, d), lambda i: (i, 0)),
                  pl.BlockSpec((DT_HALO, d), lambda i: (jnp.minimum((i + 1) * hb, nhalo - 1), 0)),
                  pl.BlockSpec((ts, d), lambda i: (i, 0)),
                  _resident(dw.shape), _resident((1, d)), _resident((1, d)), _resident((1, d)),
                  _resident(w_out.shape), _resident((1, d)), _resident((1, d)), _resident((1, d))],
        out_specs=pl.BlockSpec((ts, d), lambda i: (i, 0)),
        out_shape=jax.ShapeDtypeStruct((m, d), F32),
        scratch_shapes=[pltpu.VMEM((ts + 2 * DT_HALO, d), F32), pltpu.VMEM((ts, d), F32),
                        pltpu.VMEM((_pick(ts, 128) + 2 * DT_HALO - 8, 128), F32)],
        compiler_params=_cparams(("parallel",)),
    )(h, h, h, x, dw, _row(dw_b), _row(a_g), _row(a_b), w_out, _row(b_out), _row(g), _row(b))


def _sgu_out_kernel(u_ref, v_ref, x_ref, vg_ref, vb_ref, ws_ref, bs_ref, w_ref, bo_ref, g_ref, b_ref,
                    o_ref, z_ref, *, alpha):
    tm, d = u_ref.shape
    nh, chunk, _ = ws_ref.shape
    hw = d // nh
    vn = _ln(v_ref[...], vg_ref[...], vb_ref[...]).astype(BF16)
    for r in range(tm // chunk):
        rows = slice(r * chunk, (r + 1) * chunk)
        for hd in range(nh):
            cols = slice(hd * hw, (hd + 1) * hw)
            s = jnp.dot(ws_ref[hd], vn[rows, cols], preferred_element_type=F32) + bs_ref[hd]
            z_ref[rows, cols] = (u_ref[rows, cols] * s).astype(BF16)
    t = jnp.dot(z_ref[...], w_ref[...], preferred_element_type=F32) + bo_ref[...]
    o_ref[...] = _ln(alpha * x_ref[...] + t, g_ref[...], b_ref[...])


def _sgu_out(h, x, v_g, v_b, w_s, b_s, w_out, b_out, g, b, alpha):
    m, d = x.shape
    nh, chunk, _ = w_s.shape
    tm = 2 * chunk
    assert m % tm == 0
    bs3 = b_s.astype(F32)[:, :, None]
    return pl.pallas_call(
        functools.partial(_sgu_out_kernel, alpha=alpha),
        grid=(m // tm,),
        in_specs=[pl.BlockSpec((tm, d), lambda i: (i, 0)),
                  pl.BlockSpec((tm, d), lambda i: (i, 1)),
                  pl.BlockSpec((tm, d), lambda i: (i, 0)),
                  _resident((1, d)), _resident((1, d)), _resident(w_s.shape), _resident(bs3.shape),
                  _resident(w_out.shape), _resident((1, d)), _resident((1, d)), _resident((1, d))],
        out_specs=pl.BlockSpec((tm, d), lambda i: (i, 0)),
        out_shape=jax.ShapeDtypeStruct((m, d), F32),
        scratch_shapes=[pltpu.VMEM((tm, d), BF16)],
        compiler_params=_cparams(("parallel",)),
    )(h, h, x, _row(v_g), _row(v_b), w_s, bs3, w_out, _row(b_out), _row(g), _row(b))


def _s5_weights(a_re, a_im, b_re, b_im, c_re, c_im, log_dt):
    hp = lax.Precision.HIGHEST
    L = S5_CHUNK
    _, G, P = a_re.shape
    GW = b_re.shape[-1]
    a_re = a_re.astype(F32); a_im = a_im.astype(F32)
    dt = jnp.exp(log_dt.astype(F32))[..., None]
    n = jnp.arange(L + 1, dtype=F32)[:, None, None, None]
    mag = jnp.exp(n * (a_re * dt)[None])
    ang = n * (a_im * dt)[None]
    pw_re = mag * jnp.cos(ang)
    pw_im = mag * jnp.sin(ang)
    lb_re, lb_im = pw_re[1], pw_im[1]
    den = a_re * a_re + a_im * a_im
    n_re = lb_re - 1.0
    f_re = (n_re * a_re + lb_im * a_im) / den
    f_im = (lb_im * a_re - n_re * a_im) / den
    bb_re = f_re[..., None] * b_re - f_im[..., None] * b_im
    bb_im = f_re[..., None] * b_im + f_im[..., None] * b_re
    c_re = c_re.astype(F32); c_im = c_im.astype(F32)
    ct_re = c_re.transpose(0, 1, 3, 2)[..., None]
    ct_im = c_im.transpose(0, 1, 3, 2)[..., None]
    cb_re = ct_re * bb_re[..., None, :] - ct_im * bb_im[..., None, :]
    cb_im = ct_re * bb_im[..., None, :] + ct_im * bb_re[..., None, :]
    kern = (jnp.einsum('ndgp,dgpck->dgnck', pw_re[:L], cb_re, precision=hp)
            - jnp.einsum('ndgp,dgpck->dgnck', pw_im[:L], cb_im, precision=hp))
    s_i = jnp.arange(L)[:, None]
    t_i = jnp.arange(L)[None, :]
    lag_f = t_i - s_i
    lag_b = s_i - t_i
    tf = jnp.where((lag_f >= 0)[None, :, :, None, None], kern[0][:, jnp.clip(lag_f, 0, L - 1)], 0.0)
    tb = jnp.where((lag_b >= 0)[None, :, :, None, None], kern[1][:, jnp.clip(lag_b, 0, L - 1)], 0.0)
    toep = (tf + tb).transpose(0, 1, 4, 2, 3).reshape(G, L * GW, L * GW)
    ef = jnp.arange(L - 1, -1, -1)
    eb = jnp.arange(L)

    def inc(d, e):
        pr, pi = pw_re[e, d], pw_im[e, d]
        re = pr[..., None] * bb_re[d][None] - pi[..., None] * bb_im[d][None]
        im = pr[..., None] * bb_im[d][None] + pi[..., None] * bb_re[d][None]
        to = lambda z: z.transpose(1, 0, 3, 2).reshape(G, L * GW, P)
        return to(re), to(im)

    sf_re, sf_im = inc(0, ef)
    sb_re, sb_im = inc(1, eb)
    w1g = jnp.stack([sf_re, sf_im, sb_re, sb_im], axis=2)
    of = jnp.arange(1, L + 1)
    ob = jnp.arange(L, 0, -1)

    def outm(d, e):
        pr, pi = pw_re[e, d], pw_im[e, d]
        re = c_re[d][None] * pr[:, :, None, :] - c_im[d][None] * pi[:, :, None, :]
        im = c_re[d][None] * pi[:, :, None, :] + c_im[d][None] * pr[:, :, None, :]
        to = lambda z: z.transpose(1, 3, 0, 2).reshape(G, P, L * GW)
        return to(re), to(-im)

    mf_re, mf_im = outm(0, of)
    mb_re, mb_im = outm(1, ob)
    w2g = jnp.stack([mf_re, mf_im, mb_re, mb_im], axis=1)
    gl = S5_LANES // GW
    q = G // gl
    same = jnp.eye(gl, dtype=BF16).reshape(1, 1, gl, 1, 1, gl, 1)

    def spread(z6):
        return z6.astype(BF16).transpose(0, 2, 1, 3, 4, 5)[:, :, :, :, :, None, :] * same

    w1 = spread(w1g.reshape(q, gl, L, GW, 4, P)).reshape(q, L * gl * GW, 4 * gl * P)
    tp = spread(toep.reshape(q, gl, L, GW, L, GW)).reshape(q, L * gl * GW, L * gl * GW)
    w2s = spread(w2g.reshape(q, gl, 4, P, L, GW)).reshape(q, 4 * gl * P, L * gl * GW)
    w2 = jnp.concatenate([tp, w2s], axis=1)
    lam_g = jnp.stack([pw_re[L, 0], pw_im[L, 0], pw_re[L, 1], pw_im[L, 1]], axis=1)
    lam = lam_g.reshape(q, gl, 4, P).transpose(0, 2, 1, 3).reshape(q, 4 * gl * P)
    return w1, w2, lam


def _s5_rows(x_ref):
    return jnp.concatenate([x_ref[:, l, :].astype(BF16) for l in range(x_ref.shape[1])], axis=1)


def _s5_inc_kernel(x_ref, w_ref, o_ref):
    o_ref[...] = jnp.dot(_s5_rows(x_ref), w_ref[0], preferred_element_type=F32)


def _s5_mix_kernel(x_ref, h_ref, w_ref, o_ref):
    _, L, lanes = x_ref.shape
    kx = L * lanes
    y = jnp.dot(_s5_rows(x_ref), w_ref[0, :kx, :], preferred_element_type=F32)
    y = y + jnp.dot(h_ref[...].astype(BF16), w_ref[0, kx:, :], preferred_element_type=F32)
    for t in range(L):
        o_ref[:, t, :] = y[:, t * lanes:(t + 1) * lanes]


def _s5_scan_kernel(s_ref, lam_ref, h_ref, *, pairs, lanes):
    nc = s_ref.shape[0]
    sub = 8
    pl_w = lanes // 4
    zero = jnp.zeros((1, pl_w), F32)

    def step(kk, carry):
        rf = pl.ds(pl.multiple_of(kk * sub, sub), sub)
        rb = pl.ds(pl.multiple_of(nc - sub - kk * sub, sub), sub)
        new = []
        for j in range(pairs):
            o = j * lanes
            cf_r, cf_i = slice(o, o + pl_w), slice(o + pl_w, o + 2 * pl_w)
            cb_r, cb_i = slice(o + 2 * pl_w, o + 3 * pl_w), slice(o + 3 * pl_w, o + 4 * pl_w)
            fr, fi, br, bi = carry[4 * j:4 * j + 4]
            lfr, lfi, lbr, lbi = lam_ref[:, cf_r], lam_ref[:, cf_i], lam_ref[:, cb_r], lam_ref[:, cb_i]
            sfr, sfi, sbr, sbi = s_ref[rf, cf_r], s_ref[rf, cf_i], s_ref[rb, cb_r], s_ref[rb, cb_i]
            ofr, ofi, obr, obi = [], [], [], []
            for r in range(sub):
                ofr.append(fr)
                ofi.append(fi)
                fr, fi = lfr * fr - lfi * fi + sfr[r:r + 1], lfr * fi + lfi * fr + sfi[r:r + 1]
            for r in range(sub - 1, -1, -1):
                obr.append(br)
                obi.append(bi)
                br, bi = lbr * br - lbi * bi + sbr[r:r + 1], lbr * bi + lbi * br + sbi[r:r + 1]
            h_ref[rf, cf_r] = jnp.concatenate(ofr, axis=0)
            h_ref[rf, cf_i] = jnp.concatenate(ofi, axis=0)
            h_ref[rb, cb_r] = jnp.concatenate(obr[::-1], axis=0)
            h_ref[rb, cb_i] = jnp.concatenate(obi[::-1], axis=0)
            new += [fr, fi, br, bi]
        return tuple(new)

    lax.fori_loop(0, nc // sub, step, (zero,) * (4 * pairs))


def _s5_core(x, bsz, seq, w1, w2, lam):
    m, d = x.shape
    L = S5_CHUNK
    q, kx, ns = w1.shape
    lanes = kx // L
    nch = m // L
    nc = seq // L
    x3 = x.reshape(nch, L, d)
    tm = _pick(nch, 256)
    s = pl.pallas_call(
        _s5_inc_kernel,
        grid=(q, nch // tm),
        in_specs=[pl.BlockSpec((tm, L, lanes), lambda p, i: (i, 0, p)),
                  pl.BlockSpec((1, kx, ns), lambda p, i: (p, 0, 0))],
        out_specs=pl.BlockSpec((tm, ns), lambda p, i: (i, p)),
        out_shape=jax.ShapeDtypeStruct((nch, q * ns), F32),
        compiler_params=_cparams(("parallel", "arbitrary")),
    )(x3, w1)
    h = pl.pallas_call(
        functools.partial(_s5_scan_kernel, pairs=1, lanes=ns),
        grid=(bsz, q),
        in_specs=[pl.BlockSpec((nc, ns), lambda b, p: (b, p)),
                  pl.BlockSpec((1, ns), lambda b, p: (0, p))],
        out_specs=pl.BlockSpec((nc, ns), lambda b, p: (b, p)),
        out_shape=jax.ShapeDtypeStruct((nch, q * ns), F32),
        compiler_params=_cparams(("parallel", "arbitrary")),
    )(s, lam.reshape(1, q * ns))
    y3 = pl.pallas_call(
        _s5_mix_kernel,
        grid=(q, nch // tm),
        in_specs=[pl.BlockSpec((tm, L, lanes), lambda p, i: (i, 0, p)),
                  pl.BlockSpec((tm, ns), lambda p, i: (i, p)),
                  pl.BlockSpec((1, kx + ns, kx), lambda p, i: (p, 0, 0))],
        out_specs=pl.BlockSpec((tm, L, lanes), lambda p, i: (i, 0, p)),
        out_shape=jax.ShapeDtypeStruct((nch, L, d), F32),
        compiler_params=_cparams(("parallel", "arbitrary")),
    )(x3, h, w2)
    return y3.reshape(m, d)


def _s5_out_kernel(y_ref, x_ref, d_ref, wv_ref, wg_ref, bv_ref, bg_ref, g_ref, b_ref, o_ref, z_ref, t_ref,
                   *, alpha, nj):
    j = pl.program_id(1)
    tn = wv_ref.shape[1]

    @pl.when(j == 0)
    def _():
        z_ref[...] = jax.nn.gelu(y_ref[...] + d_ref[...] * x_ref[...]).astype(BF16)

    zb = z_ref[...]
    val = jnp.dot(zb, wv_ref[...], preferred_element_type=F32) + bv_ref[...]
    gate = jnp.dot(zb, wg_ref[...], preferred_element_type=F32) + bg_ref[...]
    t_ref[:, pl.ds(pl.multiple_of(j * tn, tn), tn)] = val * jax.nn.sigmoid(gate)

    @pl.when(j == nj - 1)
    def _():
        o_ref[...] = _ln(alpha * x_ref[...] + t_ref[...], g_ref[...], b_ref[...])


def _s5_out(y, x, dskip, w_glu, b_glu, g, b, alpha):
    m, d = x.shape
    tm, tn = _pick(m, 512), _pick(d, 512)
    nj = d // tn
    bg = _row(b_glu)
    return pl.pallas_call(
        functools.partial(_s5_out_kernel, alpha=alpha, nj=nj),
        grid=(m // tm, nj),
        in_specs=[pl.BlockSpec((tm, d), lambda i, j: (i, 0)),
                  pl.BlockSpec((tm, d), lambda i, j: (i, 0)),
                  pl.BlockSpec((1, d), lambda i, j: (0, 0)),
                  pl.BlockSpec((d, tn), lambda i, j: (0, j)),
                  pl.BlockSpec((d, tn), lambda i, j: (0, j + nj)),
                  pl.BlockSpec((1, tn), lambda i, j: (0, j)),
                  pl.BlockSpec((1, tn), lambda i, j: (0, j + nj)),
                  pl.BlockSpec((1, d), lambda i, j: (0, 0)),
                  pl.BlockSpec((1, d), lambda i, j: (0, 0))],
        out_specs=pl.BlockSpec((tm, d), lambda i, j: (i, 0)),
        out_shape=jax.ShapeDtypeStruct((m, d), F32),
        scratch_shapes=[pltpu.VMEM((tm, d), BF16), pltpu.VMEM((tm, d), F32)],
        compiler_params=_cparams(("parallel", "arbitrary")),
    )(y, x, _row(dskip), w_glu, w_glu, bg, bg, _row(g), _row(b))


def _xattn_kernel(x_ref, k_ref, v_ref, wq_ref, wo_ref, g_ref, b_ref, o_ref, a_ref, *, heads, alpha):
    x = x_ref[0]
    d = x.shape[1]
    dh = d // heads
    scale = dh ** -0.5
    xb = x.astype(BF16)
    for hd in range(heads):
        cols = slice(hd * dh, (hd + 1) * dh)
        qh = jnp.dot(xb, wq_ref[:, cols], preferred_element_type=F32).astype(BF16)
        s = lax.dot_general(qh, k_ref[0, :, cols], (((1,), (1,)), ((), ())), preferred_element_type=F32) * scale
        s = s - jnp.max(s, axis=-1, keepdims=True)
        e = jnp.exp(s)
        p = (e / jnp.sum(e, axis=-1, keepdims=True)).astype(BF16)
        a_ref[:, cols] = jnp.dot(p, v_ref[0, :, cols], preferred_element_type=F32).astype(BF16)
    t = jnp.dot(a_ref[...], wo_ref[...], preferred_element_type=F32)
    o_ref[0] = _ln(alpha * x + t, g_ref[...], b_ref[...])


def _xattn(x3, k3, v3, wq, wo, g, b, heads, alpha):
    bsz, seq, d = x3.shape
    nm = k3.shape[1]
    tm = _pick(seq, 512)
    return pl.pallas_call(
        functools.partial(_xattn_kernel, heads=heads, alpha=alpha),
        grid=(bsz, seq // tm),
        in_specs=[pl.BlockSpec((1, tm, d), lambda bi, i: (bi, i, 0)),
                  pl.BlockSpec((1, nm, d), lambda bi, i: (bi, 0, 0)),
                  pl.BlockSpec((1, nm, d), lambda bi, i: (bi, 0, 0)),
                  _resident(wq.shape), _resident(wo.shape), _resident((1, d)), _resident((1, d))],
        out_specs=pl.BlockSpec((1, tm, d), lambda bi, i: (bi, i, 0)),
        out_shape=jax.ShapeDtypeStruct((bsz, seq, d), F32),
        scratch_shapes=[pltpu.VMEM((tm, d), BF16)],
        compiler_params=_cparams(("parallel", "arbitrary")),
    )(x3, k3, v3, wq, wo, _row(g), _row(b))


def _router_kernel(x_ref, w_ref, o_ref, *, n_exp):
    logits = jnp.dot(x_ref[...], w_ref[...], preferred_element_type=F32, precision=lax.Precision.HIGHEST)
    lane = lax.broadcasted_iota(jnp.int32, logits.shape, 1)
    logits = jnp.where(lane < n_exp, logits, -jnp.inf)
    e = jnp.exp(logits - jnp.max(logits, axis=-1, keepdims=True))
    p = e / jnp.sum(e, axis=-1, keepdims=True)
    o_ref[...] = p.T[:o_ref.shape[0], :]


def _router(x, w_router):
    m, d = x.shape
    n_exp = w_router.shape[1]
    assert n_exp <= 128
    rows = -(-n_exp // 8) * 8
    wpad = jnp.zeros((d, 128), F32).at[:, :n_exp].set(w_router.astype(F32))
    tm = _pick(m, 512)
    out = pl.pallas_call(
        functools.partial(_router_kernel, n_exp=n_exp),
        grid=(m // tm,),
        in_specs=[pl.BlockSpec((tm, d), lambda i: (i, 0)), _resident((d, 128))],
        out_specs=pl.BlockSpec((rows, tm), lambda i: (0, i)),
        out_shape=jax.ShapeDtypeStruct((rows, m), F32),
        compiler_params=_cparams(("parallel",)),
    )(x, wpad)
    return out[:n_exp]


def _cumsum_tokens(mask, upper, lstrict):
    local = jnp.dot(mask.astype(BF16), upper, preferred_element_type=F32)
    tot = jnp.sum(mask, axis=1, keepdims=True)
    off = jnp.dot(lstrict, jnp.broadcast_to(tot, mask.shape).astype(BF16), preferred_element_type=F32)
    return off + local, tot


def _ec_select_kernel(p_ref, idx_ref, gate_ref, pos_ref, start_ref, end_ref, *, cap, sc):
    p = p_ref[0]
    R = p.shape[0]
    bits = pltpu.bitcast(p, jnp.int32)

    def search(i, thr):
        cand = thr | jnp.left_shift(jnp.int32(1), 30 - i)
        cnt = jnp.sum((bits >= cand).astype(F32), axis=(0, 1), keepdims=True)
        return jnp.where(cnt >= cap, cand, thr)

    thr = lax.fori_loop(0, 31, search, jnp.zeros((1, 1), jnp.int32))
    ii = lax.broadcasted_iota(jnp.int32, (128, 128), 0)
    jj = lax.broadcasted_iota(jnp.int32, (128, 128), 1)
    upper = (ii <= jj).astype(BF16)
    ri = lax.broadcasted_iota(jnp.int32, (R, R), 0)
    rj = lax.broadcasted_iota(jnp.int32, (R, R), 1)
    lstrict = (rj < ri).astype(BF16)
    uincl = (ri <= rj).astype(BF16)

    gt = (bits > thr).astype(F32)
    eq = (bits == thr).astype(F32)
    need = cap - jnp.sum(gt, axis=(0, 1), keepdims=True)
    tie_incl, _ = _cumsum_tokens(eq, upper, lstrict)
    sel = gt + eq * ((tie_incl - eq) < need).astype(F32)
    cs, _ = _cumsum_tokens(sel, upper, lstrict)
    pos_ref[0] = jnp.where(sel > 0.5, cs - 1.0, -1.0).astype(jnp.int32)

    ones8 = jnp.ones((8, 128), BF16)
    tot_row = lax.dot_general(ones8, sel.astype(BF16), (((1,), (1,)), ((), ())), preferred_element_type=F32)
    end_row = jnp.dot(tot_row.astype(BF16), uincl, preferred_element_type=F32)[0:1]
    start_row = end_row - tot_row[0:1]
    start_ref[0] = start_row.astype(jnp.int32)
    end_ref[0] = end_row.astype(jnp.int32)

    cs_hi = jnp.floor(cs * (1.0 / 64.0))
    cs_lo = cs - 64.0 * cs_hi
    p1 = p.astype(BF16)
    p2 = (p - p1.astype(F32)).astype(BF16)
    p3 = (p - p1.astype(F32) - p2.astype(F32)).astype(BF16)
    rhs = jnp.concatenate([cs_hi.astype(BF16), cs_lo.astype(BF16), sel.astype(BF16), p1, p2, p3], axis=1)
    for c in range(cap // sc):
        s_col = (lax.broadcasted_iota(jnp.int32, (sc, 1), 0) + c * sc).astype(F32)
        onehot_row = jnp.logical_and(start_row <= s_col, s_col < end_row).astype(BF16)
        g = jnp.dot(onehot_row, rhs, preferred_element_type=F32)
        csg = 64.0 * g[:, 0:128] + g[:, 128:256]
        tok_in_row = jnp.sum((csg <= s_col).astype(F32), axis=1, keepdims=True)
        row = jnp.sum((end_row <= s_col).astype(F32), axis=1, keepdims=True)
        hit = jnp.logical_and(csg == s_col + 1.0, g[:, 256:384] > 0.5)
        pg = (g[:, 640:768] + g[:, 512:640]) + g[:, 384:512]
        idx_ref[0, c * sc:(c + 1) * sc, :] = (128.0 * row + tok_in_row).astype(jnp.int32)
        gate_ref[0, c * sc:(c + 1) * sc, :] = jnp.sum(jnp.where(hit, pg, 0.0), axis=1, keepdims=True)


def _ec_select(probs_t, cap):
    n_exp, n = probs_t.shape
    assert n % 128 == 0 and cap <= 64 * 256
    r = n // 128
    sc = _pick(cap, 512)
    idx, gate, pos, start, end = pl.pallas_call(
        functools.partial(_ec_select_kernel, cap=cap, sc=sc),
        grid=(n_exp,),
        in_specs=[pl.BlockSpec((1, r, 128), lambda e: (e, 0, 0))],
        out_specs=[pl.BlockSpec((1, cap, 1), lambda e: (e, 0, 0)),
                   pl.BlockSpec((1, cap, 1), lambda e: (e, 0, 0)),
                   pl.BlockSpec((1, r, 128), lambda e: (e, 0, 0)),
                   pl.BlockSpec((1, 1, r), lambda e: (e, 0, 0)),
                   pl.BlockSpec((1, 1, r), lambda e: (e, 0, 0))],
        out_shape=[jax.ShapeDtypeStruct((n_exp, cap, 1), jnp.int32),
                   jax.ShapeDtypeStruct((n_exp, cap, 1), F32),
                   jax.ShapeDtypeStruct((n_exp, r, 128), jnp.int32),
                   jax.ShapeDtypeStruct((n_exp, 1, r), jnp.int32),
                   jax.ShapeDtypeStruct((n_exp, 1, r), jnp.int32)],
        compiler_params=_cparams(("parallel",)),
    )(probs_t.reshape(n_exp, r, 128))
    return idx.reshape(n_exp, cap), gate.reshape(n_exp, cap), pos, start.reshape(n_exp, r), end.reshape(n_exp, r)


def _moe_ffn_kernel(idx_ref, x_hbm, gate_ref, wg_ref, wu_ref, wd_ref, o_ref, xg_ref, xb_ref, sem, *, nf, ntiles):
    f = pl.program_id(2)
    tile = pl.program_id(0) * pl.num_programs(1) + pl.program_id(1)
    tm = xg_ref.shape[0]
    per = tm // nf

    def row_copy(slot, dst_row):
        return pltpu.make_async_copy(x_hbm.at[pl.ds(idx_ref[slot], 1)], xg_ref.at[pl.ds(dst_row, 1)], sem)

    @pl.when(jnp.logical_and(tile == 0, f == 0))
    def _():
        def issue(i, c):
            row_copy(i, i).start()
            return c

        lax.fori_loop(0, tm, issue, 0)

    @pl.when(f == 0)
    def _():
        pltpu.make_async_copy(x_hbm.at[pl.ds(0, tm)], xg_ref, sem).wait()
        xb_ref[...] = xg_ref[...].astype(BF16)
        o_ref[...] = jnp.zeros_like(o_ref)

    nxt = jnp.minimum(tile + 1, ntiles - 1) * tm + f * per
    for i in range(per):
        row_copy(nxt + i, f * per + i).start()

    xb = xb_ref[...]
    hg = jnp.dot(xb, wg_ref[0].astype(BF16), preferred_element_type=F32)
    hu = jnp.dot(xb, wu_ref[0].astype(BF16), preferred_element_type=F32)
    hh = (hg * jax.nn.sigmoid(hg) * hu).astype(BF16)
    o_ref[...] += jnp.dot(hh, wd_ref[0].astype(BF16), preferred_element_type=F32)

    @pl.when(f == nf - 1)
    def _():
        o_ref[...] = o_ref[...] * gate_ref[...]

    @pl.when(jnp.logical_and(tile == ntiles - 1, f == nf - 1))
    def _():
        pltpu.make_async_copy(x_hbm.at[pl.ds(0, tm)], xg_ref, sem).wait()


def _moe_ffn(x, idx, gate, w_gate, w_up, w_down):
    n, d = x.shape
    n_exp, cap = idx.shape
    dff = w_gate.shape[2]
    tm, tf = _pick(cap, 1024), _pick(dff, 256)
    nr, nf = cap // tm, dff // tf
    grid_spec = pltpu.PrefetchScalarGridSpec(
        num_scalar_prefetch=1,
        grid=(n_exp, nr, nf),
        in_specs=[pl.BlockSpec(memory_space=pl.ANY),
                  pl.BlockSpec((tm, 1), lambda e, r, f, idx: (e * nr + r, 0)),
                  pl.BlockSpec((1, d, tf), lambda e, r, f, idx: (e, 0, f)),
                  pl.BlockSpec((1, d, tf), lambda e, r, f, idx: (e, 0, f)),
                  pl.BlockSpec((1, tf, d), lambda e, r, f, idx: (e, f, 0))],
        out_specs=pl.BlockSpec((tm, d), lambda e, r, f, idx: (e * nr + r, 0)),
        scratch_shapes=[pltpu.VMEM((tm, d), F32), pltpu.VMEM((tm, d), BF16), pltpu.SemaphoreType.DMA(())],
    )
    return pl.pallas_call(
        functools.partial(_moe_ffn_kernel, nf=nf, ntiles=n_exp * nr),
        grid_spec=grid_spec,
        out_shape=jax.ShapeDtypeStruct((n_exp * cap, d), F32),
        compiler_params=_cparams(("arbitrary", "arbitrary", "arbitrary")),
    )(idx.reshape(-1).astype(jnp.int32), x, gate.reshape(-1, 1).astype(F32), w_gate, w_up, w_down)


COMBINE_PIECE = 8
COMBINE_K = 512


def _combine_kernel(first_ref, npiece_ref, ye_hbm, pos_ref, x_ref, g_ref, b_ref, o_ref, stage_ref, acc_ref, sem,
                    *, n_exp, cap, alpha):
    r = pl.program_id(0)
    tt = x_ref.shape[0]
    slot = r % 2

    def piece_copy(src_row, s, dst_row):
        return pltpu.make_async_copy(
            ye_hbm.at[pl.ds(pl.multiple_of(src_row, COMBINE_PIECE), COMBINE_PIECE)],
            stage_ref.at[s, pl.ds(pl.multiple_of(dst_row, COMBINE_PIECE), COMBINE_PIECE)], sem.at[s])

    def start_row(rr, s):
        base = jnp.int32(0)
        for e in range(n_exp):
            first = first_ref[rr * n_exp + e]
            npiece = npiece_ref[rr * n_exp + e]

            def issue(p, c, first=first, base=base, e=e):
                piece_copy(e * cap + first + p * COMBINE_PIECE, s, base + p * COMBINE_PIECE).start()
                return c

            lax.fori_loop(0, npiece, issue, 0)
            base = base + npiece * COMBINE_PIECE

    @pl.when(r == 0)
    def _():
        stage_ref[...] = jnp.zeros_like(stage_ref)
        start_row(0, 0)

    @pl.when(r + 1 < pl.num_programs(0))
    def _():
        start_row(r + 1, 1 - slot)

    base = jnp.int32(0)
    kpos = []
    for e in range(n_exp):
        first = first_ref[r * n_exp + e]
        pe = pos_ref[0, e:e + 1, :]
        kpos.append(jnp.where(pe >= 0, pe - first + base, -1))
        base = base + npiece_ref[r * n_exp + e] * COMBINE_PIECE
    total = base

    def drain(p, c):
        piece_copy(0, slot, 0).wait()
        return c

    lax.fori_loop(0, total // COMBINE_PIECE, drain, 0)
    acc_ref[...] = jnp.zeros_like(acc_ref)

    def chunk(c, carry):
        k0 = pl.multiple_of(c * COMBINE_K, COMBINE_K)
        kio = lax.broadcasted_iota(jnp.int32, (COMBINE_K, tt), 0) + k0
        hit = kio == kpos[0]
        for e in range(1, n_exp):
            hit = jnp.logical_or(hit, kio == kpos[e])
        onehot = jnp.where(hit, 1.0, 0.0).astype(BF16)
        st = stage_ref[slot, pl.ds(k0, COMBINE_K), :]
        hi = st.astype(BF16)
        lo = (st - hi.astype(F32)).astype(BF16)
        dn = (((0,), (0,)), ((), ()))
        acc_ref[...] += (lax.dot_general(onehot, hi, dn, preferred_element_type=F32)
                         + lax.dot_general(onehot, lo, dn, preferred_element_type=F32))
        return carry

    lax.fori_loop(0, (total + COMBINE_K - 1) // COMBINE_K, chunk, 0)
    o_ref[...] = _ln(alpha * x_ref[...] + acc_ref[...], g_ref[...], b_ref[...])


def _combine(ye, pos, start, end, x, g, b, alpha):
    n, d = x.shape
    n_exp, r, tt = pos.shape
    cap = ye.shape[0] // n_exp
    assert cap % COMBINE_PIECE == 0
    first = (start // COMBINE_PIECE) * COMBINE_PIECE
    npiece = jnp.where(end > start, (end - first + COMBINE_PIECE - 1) // COMBINE_PIECE, 0)
    stage_rows = n_exp * (tt + COMBINE_PIECE)
    stage_rows = -(-stage_rows // COMBINE_K) * COMBINE_K
    grid_spec = pltpu.PrefetchScalarGridSpec(
        num_scalar_prefetch=2,
        grid=(r,),
        in_specs=[pl.BlockSpec(memory_space=pl.ANY),
                  pl.BlockSpec((1, n_exp, tt), lambda i, a, c: (i, 0, 0)),
                  pl.BlockSpec((tt, d), lambda i, a, c: (i, 0)),
                  pl.BlockSpec((1, d), lambda i, a, c: (0, 0)),
                  pl.BlockSpec((1, d), lambda i, a, c: (0, 0))],
        out_specs=pl.BlockSpec((tt, d), lambda i, a, c: (i, 0)),
        scratch_shapes=[pltpu.VMEM((2, stage_rows, d), F32), pltpu.VMEM((tt, d), F32),
                        pltpu.SemaphoreType.DMA((2,))],
    )
    return pl.pallas_call(
        functools.partial(_combine_kernel, n_exp=n_exp, cap=cap, alpha=alpha),
        grid_spec=grid_spec,
        out_shape=jax.ShapeDtypeStruct((n, d), F32),
        compiler_params=_cparams(("arbitrary",)),
    )(first.T.reshape(-1).astype(jnp.int32), npiece.T.reshape(-1).astype(jnp.int32), ye,
      pos.transpose(1, 0, 2), x, _row(g), _row(b))


def _ec_moe(x, probs_t, w_gate, w_up, w_down, g, b, alpha, ec_capacity):
    n, d = x.shape
    n_exp = probs_t.shape[0]
    cap = ec_capacity * n // n_exp
    idx, gate, pos, start, end = _ec_select(probs_t, cap)
    ye = _moe_ffn(x, idx, gate, w_gate, w_up, w_down)
    return _combine(ye, pos, start, end, x, g, b, alpha)


EC_CAPACITY = 2
N_XHEADS = 4
N_MIXERS = 3


def _trunk(x3, mem3, p):
    bsz, seq, d = x3.shape
    m = bsz * seq
    depth = p['ln_g'].shape[0]
    alpha = (2.0 * depth) ** 0.25
    x = x3.reshape(m, d)
    mem = mem3.reshape(-1, d)
    for i in range(depth):
        mix, j = i % N_MIXERS, i // N_MIXERS
        g, b = p['ln_g'][i], p['ln_b'][i]
        if mix == 0:
            h = _glu_in(x, p['a_w_in'][j].astype(BF16), p['a_b_in'][j])
            x = _conv_out(h, x, seq, p['a_dw'][j].astype(F32), p['a_dw_b'][j], p['a_ln_g'][j], p['a_ln_b'][j],
                          p['a_w_out'][j].astype(BF16), p['a_b_out'][j], g[0], b[0], alpha)
        elif mix == 1:
            h = _gelu_in(x, p['b_w_in'][j].astype(BF16), p['b_b_in'][j])
            x = _sgu_out(h, x, p['b_ln_g'][j], p['b_ln_b'][j], p['b_w_s'][j].astype(BF16), p['b_b_s'][j],
                         p['b_w_out'][j].astype(BF16), p['b_b_out'][j], g[0], b[0], alpha)
        else:
            w1, w2, lam = _s5_weights(p['c_a_re'][j], p['c_a_im'][j], p['c_b_re'][j], p['c_b_im'][j],
                                      p['c_c_re'][j], p['c_c_im'][j], p['c_log_dt'][j])
            y = _s5_core(x, bsz, seq, w1, w2, lam)
            x = _s5_out(y, x, p['c_d'][j], p['c_w_glu'][j].astype(BF16), p['c_b_glu'][j], g[0], b[0], alpha)
        wkv = jnp.concatenate([p['x_wk'][i], p['x_wv'][i]], axis=1).astype(BF16)
        kv = _mm(mem, wkv, BF16).reshape(bsz, -1, 2 * d)
        x = _xattn(x.reshape(bsz, seq, d), kv[:, :, :d], kv[:, :, d:], p['x_wq'][i].astype(BF16),
                   p['x_wo'][i].astype(BF16), g[1], b[1], N_XHEADS, alpha).reshape(m, d)
        x = _ec_moe(x, _router(x, p['e_w_router'][i]), p['e_w_gate'][i], p['e_w_up'][i], p['e_w_down'][i],
                    g[2], b[2], alpha, EC_CAPACITY)
    return x.reshape(bsz, seq, d)


def kernel(x_prompt, x_sample, mem_prompt, mem_sample, a_w_in, a_b_in, a_dw, a_dw_b, a_ln_g, a_ln_b, a_w_out, a_b_out, b_w_in, b_b_in, b_ln_g, b_ln_b, b_w_s, b_b_s, b_w_out, b_b_out, c_a_re, c_a_im, c_b_re, c_b_im, c_c_re, c_c_im, c_log_dt, c_d, c_w_glu, c_b_glu, x_wq, x_wk, x_wv, x_wo, e_w_router, e_w_gate, e_w_up, e_w_down, ln_g, ln_b):
    p = dict(a_w_in=a_w_in, a_b_in=a_b_in, a_dw=a_dw, a_dw_b=a_dw_b, a_ln_g=a_ln_g, a_ln_b=a_ln_b,
             a_w_out=a_w_out, a_b_out=a_b_out,
             b_w_in=b_w_in, b_b_in=b_b_in, b_ln_g=b_ln_g, b_ln_b=b_ln_b, b_w_s=b_w_s, b_b_s=b_b_s,
             b_w_out=b_w_out, b_b_out=b_b_out,
             c_a_re=c_a_re, c_a_im=c_a_im, c_b_re=c_b_re, c_b_im=c_b_im, c_c_re=c_c_re, c_c_im=c_c_im,
             c_log_dt=c_log_dt, c_d=c_d, c_w_glu=c_w_glu, c_b_glu=c_b_glu,
             x_wq=x_wq, x_wk=x_wk, x_wv=x_wv, x_wo=x_wo,
             e_w_router=e_w_router, e_w_gate=e_w_gate, e_w_up=e_w_up, e_w_down=e_w_down,
             ln_g=ln_g, ln_b=ln_b)
    return (_trunk(x_prompt, mem_prompt, p), _trunk(x_sample, mem_sample, p))
```
